```python
import math
import jax, jax.numpy as jnp
from jax import lax
import numpy as np

D_MODEL = 1024
BATCH = 4
SEQ = 8192
DEPTH = 2

BLOCK = 128
MLA_HEADS = 8
MLA_NOPE = 64
MLA_ROPE = 32
MLA_V = 64
MLA_Q_RANK = 256
MLA_KV_RANK = 128
ROPE_BASE = 10000.0
CONV_CH = 256
CONV_GROUPS = 4
CONV_WIDTH = 31
SWA_HEADS = 4
SWA_KV_HEADS = 2
SWA_HEAD_DIM = 64
WINDOW = 128
MLA_OUT = MLA_HEADS * MLA_V
SWA_OUT = SWA_HEADS * SWA_HEAD_DIM
D_MIX = MLA_OUT + CONV_CH + SWA_OUT
IN_SPLITS = (MLA_Q_RANK, MLA_KV_RANK, MLA_ROPE, 2 * CONV_CH,
             SWA_HEADS * SWA_HEAD_DIM, SWA_KV_HEADS * SWA_HEAD_DIM, SWA_KV_HEADS * SWA_HEAD_DIM)
D_IN = 256 + 128 + 32 + 512 + 256 + 128 + 128
N_SUBKEYS = 128
N_EXPERTS = N_SUBKEYS * N_SUBKEYS
PEER_HEADS = 8
PEER_QDIM = 256
PEER_TOPK = 16
DN_ALPHA = (2 * DEPTH) ** 0.25
DN_BETA = (8 * DEPTH) ** -0.25
EPS = 1e-6
ADA_SCALE = 0.1

kernel_name = "hybrid_mla_conformer_swa_peer_deepnorm"


def layer_norm(x, g, b):
    xf = x.astype(jnp.float32)
    mu = jnp.mean(xf, axis=-1, keepdims=True)
    var = jnp.mean(jnp.square(xf - mu), axis=-1, keepdims=True)
    return ((xf - mu) * lax.rsqrt(var + EPS)).astype(x.dtype) * g + b


def rms_norm(x, g):
    xf = x.astype(jnp.float32)
    return (xf * lax.rsqrt(jnp.mean(xf * xf, axis=-1, keepdims=True) + EPS)).astype(x.dtype) * g


def rope_tables(seq, dim, dtype):
    inv = 1.0 / (ROPE_BASE ** (jnp.arange(0, dim, 2, dtype=jnp.float32) / dim))
    ang = jnp.arange(seq, dtype=jnp.float32)[:, None] * inv[None, :]
    return jnp.cos(ang).astype(dtype), jnp.sin(ang).astype(dtype)


def apply_rope(t, cos, sin):
    half = t.shape[-1] // 2
    t1, t2 = t[..., :half], t[..., half:]
    return jnp.concatenate([t1 * cos - t2 * sin, t1 * sin + t2 * cos], axis=-1)


def mla_attention(c_q, c_kv, k_rope, g_q, w_uq, g_kv, w_ukv, cos, sin):
    B, S, _ = c_q.shape
    q = (rms_norm(c_q, g_q) @ w_uq).reshape(B, S, MLA_HEADS, MLA_NOPE + MLA_ROPE)
    q_nope = q[..., :MLA_NOPE]
    q_rope = apply_rope(q[..., MLA_NOPE:], cos[:, None, :], sin[:, None, :])
    k_rope = apply_rope(k_rope, cos, sin)
    kv = (rms_norm(c_kv, g_kv) @ w_ukv).reshape(B, S, MLA_HEADS, MLA_NOPE + MLA_V)
    k_nope, v = kv[..., :MLA_NOPE], kv[..., MLA_NOPE:]
    scale = (MLA_NOPE + MLA_ROPE) ** -0.5
    nb = S // BLOCK
    qn_b = q_nope.reshape(B, nb, BLOCK, MLA_HEADS, MLA_NOPE).transpose(1, 0, 2, 3, 4)
    qr_b = q_rope.reshape(B, nb, BLOCK, MLA_HEADS, MLA_ROPE).transpose(1, 0, 2, 3, 4)
    kpos = jnp.arange(S)

    def one_block(args):
        qn, qr, n = args
        s = (jnp.einsum('bqhd,bkhd->bhqk', qn, k_nope)
             + jnp.einsum('bqhr,bkr->bhqk', qr, k_rope)).astype(jnp.float32) * scale
        qpos = n * BLOCK + jnp.arange(BLOCK)
        s = jnp.where(kpos[None, :] <= qpos[:, None], s, -jnp.inf)
        p = jax.nn.softmax(s, axis=-1).astype(v.dtype)
        return jnp.einsum('bhqk,bkhd->bqhd', p, v)

    out = lax.map(one_block, (qn_b, qr_b, jnp.arange(nb)))
    return out.transpose(1, 0, 2, 3, 4).reshape(B, S, MLA_OUT)


def conformer_conv(u, w_dw, b_dw, g_ln, b_ln, w_pw, b_pw):
    a, gate = jnp.split(u, 2, axis=-1)
    h = a * jax.nn.sigmoid(gate)
    h = lax.conv_general_dilated(h, w_dw[:, None, :], window_strides=(1,),
                                 padding=[(CONV_WIDTH - 1, 0)],
                                 dimension_numbers=('NWC', 'WIO', 'NWC'),
                                 feature_group_count=CONV_CH) + b_dw
    h = jax.nn.silu(layer_norm(h, g_ln, b_ln))
    return h @ w_pw + b_pw


def swa_attention(q, k, v, sinks):
    B, S, _ = q.shape
    nb = S // BLOCK
    G = SWA_HEADS // SWA_KV_HEADS
    q = q.reshape(B, nb, BLOCK, SWA_KV_HEADS, G, SWA_HEAD_DIM)
    k = k.reshape(B, nb, BLOCK, SWA_KV_HEADS, SWA_HEAD_DIM)
    v = v.reshape(B, nb, BLOCK, SWA_KV_HEADS, SWA_HEAD_DIM)

    def with_prev(t):
        prev = jnp.pad(t, ((0, 0), (1, 0), (0, 0), (0, 0), (0, 0)))[:, :-1]
        return jnp.concatenate([prev, t], axis=2)

    kb, vb = with_prev(k), with_prev(v)
    s = jnp.einsum('bnqhgd,bnkhd->bnhgqk', q, kb).astype(jnp.float32) * (SWA_HEAD_DIM ** -0.5)
    i = jnp.arange(BLOCK)[:, None]
    j = jnp.arange(2 * BLOCK)[None, :]
    rel = i + BLOCK - j
    band = (rel >= 0) & (rel < WINDOW)
    valid = (jnp.arange(nb)[:, None, None] > 0) | (j >= BLOCK)[None]
    mask = band[None] & valid
    s = jnp.where(mask[None, :, None, None], s, -jnp.inf)
    sink = sinks.astype(jnp.float32).reshape(SWA_KV_HEADS, G)[None, None, :, :, None, None]
    m = jnp.maximum(jnp.max(s, axis=-1, keepdims=True), sink)
    p = jnp.exp(s - m)
    p = (p / (jnp.sum(p, axis=-1, keepdims=True) + jnp.exp(sink - m))).astype(vb.dtype)
    o = jnp.einsum('bnhgqk,bnkhd->bnqhgd', p, vb)
    return o.reshape(B, S, SWA_OUT)


def peer(h, w_pq, sub_keys, u_tab, v_tab):
    B, S, D = h.shape
    q = (h @ w_pq).reshape(B, S, PEER_HEADS, 2, PEER_QDIM // 2)
    sc = jnp.einsum('bshpd,hpnd->bshpn', q, sub_keys).astype(jnp.float32)
    s1, i1 = lax.top_k(sc[..., 0, :], PEER_TOPK)
    s2, i2 = lax.top_k(sc[..., 1, :], PEER_TOPK)
    cand = (s1[..., :, None] + s2[..., None, :]).reshape(B, S, PEER_HEADS, PEER_TOPK * PEER_TOPK)
    cidx = (i1[..., :, None] * N_SUBKEYS + i2[..., None, :]).reshape(B, S, PEER_HEADS, PEER_TOPK * PEER_TOPK)
    top_s, pos = lax.top_k(cand, PEER_TOPK)
    eidx = jnp.take_along_axis(cidx, pos, axis=-1)
    gate = jax.nn.softmax(top_s, axis=-1).astype(h.dtype)
    nb = S // BLOCK
    HK = PEER_HEADS * PEER_TOPK
    hb = h.reshape(B, nb, BLOCK, D).transpose(1, 0, 2, 3)
    eb = eidx.reshape(B, nb, BLOCK, HK).transpose(1, 0, 2, 3)
    gb = gate.reshape(B, nb, BLOCK, HK).transpose(1, 0, 2, 3)

    def one_block(args):
        xb, e, g = args
        a = jax.nn.gelu(jnp.einsum('bqd,bqkd->bqk', xb, u_tab[e]), approximate=False)
        return jnp.einsum('bqk,bqkd->bqd', g * a, v_tab[e])

    out = lax.map(one_block, (hb, eb, gb))
    return out.transpose(1, 0, 2, 3).reshape(B, S, D)


def setup_inputs(seed: int = 0) -> dict:
    key = jax.random.key(seed)
    ks = jax.random.split(key, 32)
    f32 = jnp.float32
    L, D = DEPTH, D_MODEL

    def nrm(k, shape, scale):
        return jax.random.normal(k, shape, f32) * scale

    def gain(k, shape):
        return 1.0 + 0.02 * jax.random.normal(k, shape, f32)

    return {
        "x": nrm(ks[0], (BATCH, SEQ, D), 1.0),
        "c": nrm(ks[1], (BATCH, D), 1.0),
        "w_ada": nrm(ks[2], (L, D, 6 * D), ADA_SCALE * D ** -0.5),
        "b_ada": nrm(ks[3], (L, 6 * D), 0.02),
        "w_in": nrm(ks[4], (L, D, D_IN), D ** -0.5),
        "g_q": gain(ks[5], (L, MLA_Q_RANK)),
        "w_uq": nrm(ks[6], (L, MLA_Q_RANK, MLA_HEADS * (MLA_NOPE + MLA_ROPE)), MLA_Q_RANK ** -0.5),
        "g_kv": gain(ks[7], (L, MLA_KV_RANK)),
        "w_ukv": nrm(ks[8], (L, MLA_KV_RANK, MLA_HEADS * (MLA_NOPE + MLA_V)), MLA_KV_RANK ** -0.5),
        "w_dw": nrm(ks[9], (L, CONV_WIDTH, CONV_CH), CONV_WIDTH ** -0.5),
        "b_dw": nrm(ks[10], (L, CONV_CH), 0.02),
        "g_conv": gain(ks[11], (L, CONV_CH)),
        "b_conv": nrm(ks[12], (L, CONV_CH), 0.02),
        "w_pw": nrm(ks[13], (L, CONV_CH, CONV_CH), CONV_CH ** -0.5),
        "b_pw": nrm(ks[14], (L, CONV_CH), 0.02),
        "sinks": nrm(ks[15], (L, SWA_HEADS), 1.0),
        "w_o": nrm(ks[16], (L, D_MIX, D), DN_BETA * D_MIX ** -0.5),
        "g_ln1": gain(ks[17], (L, D)),
        "b_ln1": nrm(ks[18], (L, D), 0.02),
        "w_pq": nrm(ks[19], (L, D, PEER_HEADS * PEER_QDIM), D ** -0.5),
        "sub_keys": nrm(ks[20], (L, PEER_HEADS, 2, N_SUBKEYS, PEER_QDIM // 2), (PEER_QDIM // 2) ** -0.5),
        "u_tab": nrm(ks[21], (L, N_EXPERTS, D), D ** -0.5),
        "v_tab": nrm(ks[22], (L, N_EXPERTS, D), DN_BETA),
        "g_ln2": gain(ks[23], (L, D)),
        "b_ln2": nrm(ks[24], (L, D), 0.02),
    }


def reference(x, c, w_ada, b_ada, w_in, g_q, w_uq, g_kv, w_ukv, w_dw, b_dw, g_conv, b_conv,
              w_pw, b_pw, sinks, w_o, g_ln1, b_ln1, w_pq, sub_keys, u_tab, v_tab, g_ln2, b_ln2):
    B, S, _ = x.shape
    cos, sin = rope_tables(S, MLA_ROPE, x.dtype)
    cs = jax.nn.silu(c)
    offsets = [int(o) for o in np.cumsum(IN_SPLITS)[:-1]]
    for l in range(DEPTH):
        mod = cs @ w_ada[l] + b_ada[l]
        sh1, sc1, gt1, sh2, sc2, gt2 = [m[:, None, :] for m in jnp.split(mod, 6, axis=-1)]

        h = x * (1.0 + sc1) + sh1
        proj = h @ w_in[l]
        c_q, c_kv, k_r, conv_u, q_s, k_s, v_s = jnp.split(proj, offsets, axis=-1)
        o_mla = mla_attention(c_q, c_kv, k_r, g_q[l], w_uq[l], g_kv[l], w_ukv[l], cos, sin)
        o_conv = conformer_conv(conv_u, w_dw[l], b_dw[l], g_conv[l], b_conv[l], w_pw[l], b_pw[l])
        o_swa = swa_attention(q_s, k_s, v_s, sinks[l])
        mix = jnp.concatenate([o_mla, o_conv, o_swa], axis=-1) @ w_o[l]
        x = layer_norm(DN_ALPHA * x + (1.0 + gt1) * mix, g_ln1[l], b_ln1[l])

        h = x * (1.0 + sc2) + sh2
        y = peer(h, w_pq[l], sub_keys[l], u_tab[l], v_tab[l])
        x = layer_norm(DN_ALPHA * x + (1.0 + gt2) * y, g_ln2[l], b_ln2[l])
    return x
```

```python
import functools

import jax
import jax.numpy as jnp
from jax import lax
from jax.experimental import pallas as pl
from jax.experimental.pallas import tpu as pltpu

F32 = jnp.float32
BF16 = jnp.bfloat16
I32 = jnp.int32

LANE = 128
SUBLANE = 8
VMEM_BYTES_V7X = 64 * 1024 * 1024
VMEM_LIMIT = 48 * 1024 * 1024
VMEM_LIMIT_TABLE = 56 * 1024 * 1024

MLA_HEADS = 8
MLA_NOPE = 64
MLA_ROPE = 32
MLA_V = 64
MLA_Q_RANK = 256
MLA_KV_RANK = 128
ROPE_BASE = 10000.0
CONV_CH = 256
CONV_WIDTH = 31
SWA_HEADS = 4
SWA_KV_HEADS = 2
SWA_HEAD_DIM = 64
BLOCK = 128
IN_SPLITS = (MLA_Q_RANK, MLA_KV_RANK, MLA_ROPE, 2 * CONV_CH,
             SWA_HEADS * SWA_HEAD_DIM, SWA_KV_HEADS * SWA_HEAD_DIM, SWA_KV_HEADS * SWA_HEAD_DIM)
N_SUBKEYS = 128
PEER_HEADS = 8
PEER_TOPK = 16
PEER_HK = PEER_HEADS * PEER_TOPK
EPS = 1e-6

TQ_PROJ = 512
TQ_ATT = 512
TS_SWA = 512
TS_CONV = 512
CONV_HALO = 32
TT_ROUTE = 256
TN_PEER = 256
ROWS_PER_EXPERT = 4

_NT = (((1,), (1,)), ((), ()))


def _cparams(sem, limit=VMEM_LIMIT):
    return pltpu.CompilerParams(dimension_semantics=sem, vmem_limit_bytes=limit)


def _layer_norm(v, g, b):
    mu = jnp.mean(v, axis=-1, keepdims=True)
    d = v - mu
    var = jnp.mean(d * d, axis=-1, keepdims=True)
    return d * lax.rsqrt(var + EPS) * g + b


def _rms_norm(v, g):
    return v * lax.rsqrt(jnp.mean(v * v, axis=-1, keepdims=True) + EPS) * g


def _ada_body(c_ref, w_ref, b_ref, o_ref):
    c = c_ref[...]
    cs = c * jax.nn.sigmoid(c)
    o_ref[0] = jnp.dot(cs, w_ref[0], preferred_element_type=F32,
                       precision=lax.Precision.HIGHEST) + b_ref[0]


def _ada_mod(c, w_ada, b_ada):
    depth, d, d6 = w_ada.shape
    bp = -(-c.shape[0] // SUBLANE) * SUBLANE
    cp = jnp.pad(c, ((0, bp - c.shape[0]), (0, 0)))
    tn = d
    out = pl.pallas_call(
        _ada_body,
        grid=(depth, d6 // tn),
        in_specs=[pl.BlockSpec((bp, d), lambda l, j: (0, 0)),
                  pl.BlockSpec((1, d, tn), lambda l, j: (l, 0, j)),
                  pl.BlockSpec((1, 1, tn), lambda l, j: (l, 0, j))],
        out_specs=pl.BlockSpec((1, bp, tn), lambda l, j: (l, 0, j)),
        out_shape=jax.ShapeDtypeStruct((depth, bp, d6), F32),
        compiler_params=_cparams(("arbitrary", "arbitrary")),
        name="ada_mod",
    )(cp, w_ada, b_ada.reshape(depth, 1, d6))
    return out[:, :c.shape[0]]


def _inproj_body(x_ref, sc_ref, sh_ref, win_ref, gq_ref, wq_ref, gkv_ref, wkv_ref, cos_ref, sin_ref,
                 q_ref, k_ref, v_ref, g_ref, qs_ref, ks_ref, vs_ref, *, mla_scale, swa_scale):
    hp = MLA_HEADS * LANE
    h = x_ref[0] * (1.0 + sc_ref[0]) + sh_ref[0]
    proj = jnp.dot(h.astype(BF16), win_ref[...], preferred_element_type=F32)
    o = 0
    c_q = proj[:, o:o + MLA_Q_RANK]; o += MLA_Q_RANK
    c_kv = proj[:, o:o + MLA_KV_RANK]; o += MLA_KV_RANK
    kr = proj[:, o:o + LANE]; o += LANE
    kr_sw = proj[:, o:o + LANE]; o += LANE
    a = proj[:, o:o + CONV_CH]; o += CONV_CH
    gate = proj[:, o:o + CONV_CH]; o += CONV_CH
    qs = proj[:, o:o + SWA_HEADS * LANE]; o += SWA_HEADS * LANE
    ks = proj[:, o:o + SWA_KV_HEADS * LANE]; o += SWA_KV_HEADS * LANE
    vs = proj[:, o:o + SWA_KV_HEADS * LANE]

    cos = cos_ref[...]
    sin = sin_ref[...]
    nq = _rms_norm(c_q, gq_ref[...]).astype(BF16)
    q2 = jnp.dot(nq, wq_ref[...], preferred_element_type=F32)
    nkv = _rms_norm(c_kv, gkv_ref[...]).astype(BF16)
    kv2 = jnp.dot(nkv, wkv_ref[...], preferred_element_type=F32)
    k_rope = kr * cos + kr_sw * sin
    for hd in range(MLA_HEADS):
        sl = slice(hd * LANE, (hd + 1) * LANE)
        sl2 = slice(hp + hd * LANE, hp + (hd + 1) * LANE)
        q_ref[0, :, sl] = ((q2[:, sl] * cos + q2[:, sl2] * sin) * mla_scale).astype(BF16)
        k_ref[0, :, sl] = (kv2[:, sl] + k_rope).astype(BF16)
    v_ref[0] = kv2[:, hp:].astype(BF16)
    g_ref[0] = a * jax.nn.sigmoid(gate)
    qs_ref[0] = (qs * swa_scale).astype(BF16)
    ks_ref[0] = ks.astype(BF16)
    vs_ref[0] = vs.astype(BF16)


def _pad_heads(w, heads, width):
    r = w.shape[0]
    w = w.reshape(r, heads, width)
    return jnp.pad(w, ((0, 0), (0, 0), (0, LANE - width))).reshape(r, heads * LANE)


def _rope_place(r1, r2):
    z64 = jnp.zeros(r1.shape[:-1] + (MLA_NOPE,), r1.dtype)
    z32 = jnp.zeros(r1.shape[:-1] + (LANE - MLA_NOPE - MLA_ROPE,), r1.dtype)
    return jnp.concatenate([z64, r1, r2, z32], axis=-1)


def _prep_inproj_weights(w_in, w_uq, w_ukv):
    d = w_in.shape[0]
    half = MLA_ROPE // 2
    offs = [0]
    for s in IN_SPLITS:
        offs.append(offs[-1] + s)
    c_q, c_kv, k_r, conv, q_s, k_s, v_s = [w_in[:, offs[i]:offs[i + 1]] for i in range(7)]
    kr1, kr2 = k_r[:, :half], k_r[:, half:]
    w_in_p = jnp.concatenate([
        c_q, c_kv, _rope_place(kr1, kr2), _rope_place(-kr2, kr1), conv,
        _pad_heads(q_s, SWA_HEADS, SWA_HEAD_DIM), _pad_heads(k_s, SWA_KV_HEADS, SWA_HEAD_DIM),
        _pad_heads(v_s, SWA_KV_HEADS, SWA_HEAD_DIM)], axis=1).astype(BF16)

    wq = w_uq.reshape(MLA_Q_RANK, MLA_HEADS, MLA_NOPE + MLA_ROPE)
    nope, r1, r2 = wq[..., :MLA_NOPE], wq[..., MLA_NOPE:MLA_NOPE + half], wq[..., MLA_NOPE + half:]
    zpad = jnp.zeros(nope.shape[:-1] + (LANE - MLA_NOPE - MLA_ROPE,), nope.dtype)
    placed = jnp.concatenate([nope, r1, r2, zpad], axis=-1).reshape(MLA_Q_RANK, MLA_HEADS * LANE)
    partner = _rope_place(-r2, r1).reshape(MLA_Q_RANK, MLA_HEADS * LANE)
    w_q2 = jnp.concatenate([placed, partner], axis=1).astype(BF16)

    wkv = w_ukv.reshape(MLA_KV_RANK, MLA_HEADS, MLA_NOPE + MLA_V)
    kn = jnp.pad(wkv[..., :MLA_NOPE], ((0, 0), (0, 0), (0, LANE - MLA_NOPE)))
    vv = jnp.pad(wkv[..., MLA_NOPE:], ((0, 0), (0, 0), (0, LANE - MLA_V)))
    w_kv2 = jnp.concatenate([kn.reshape(MLA_KV_RANK, -1), vv.reshape(MLA_KV_RANK, -1)], axis=1).astype(BF16)
    return w_in_p, w_q2, w_kv2


def _rope_tables(seq):
    half = MLA_ROPE // 2
    inv = 1.0 / (ROPE_BASE ** (jnp.arange(0, MLA_ROPE, 2, dtype=F32) / MLA_ROPE))
    ang = jnp.arange(seq, dtype=F32)[:, None] * inv[None, :]
    cos, sin = jnp.cos(ang), jnp.sin(ang)
    ones = jnp.ones((seq, MLA_NOPE), F32)
    zpad = jnp.zeros((seq, LANE - MLA_NOPE - MLA_ROPE), F32)
    cos_t = jnp.concatenate([ones, cos, cos, zpad], axis=1)
    sin_t = jnp.concatenate([jnp.zeros((seq, MLA_NOPE), F32), sin, sin, zpad], axis=1)
    assert half * 2 == MLA_ROPE
    return cos_t, sin_t


def _in_projection(x, sc1, sh1, w_in_p, g_q, w_q2, g_kv, w_kv2, cos_t, sin_t):
    b, s, d = x.shape
    tq = TQ_PROJ
    ncol = w_in_p.shape[1]
    hp = MLA_HEADS * LANE
    full = lambda shape: pl.BlockSpec(shape, lambda bi, i: (0,) * len(shape))
    tok = lambda w: pl.BlockSpec((1, tq, w), lambda bi, i: (bi, i, 0))
    vec = pl.BlockSpec((1, 1, d), lambda bi, i: (bi, 0, 0))
    body = functools.partial(_inproj_body, mla_scale=float((MLA_NOPE + MLA_ROPE) ** -0.5),
                             swa_scale=float(SWA_HEAD_DIM ** -0.5))
    return pl.pallas_call(
        body,
        grid=(b, s // tq),
        in_specs=[tok(d), vec, vec, full((d, ncol)), full((1, MLA_Q_RANK)), full((MLA_Q_RANK, 2 * hp)),
                  full((1, MLA_KV_RANK)), full((MLA_KV_RANK, 2 * hp)),
                  pl.BlockSpec((tq, LANE), lambda bi, i: (i, 0)),
                  pl.BlockSpec((tq, LANE), lambda bi, i: (i, 0))],
        out_specs=[tok(hp), tok(hp), tok(hp), tok(CONV_CH), tok(SWA_HEADS * LANE),
                   tok(SWA_KV_HEADS * LANE), tok(SWA_KV_HEADS * LANE)],
        out_shape=[jax.ShapeDtypeStruct((b, s, hp), BF16), jax.ShapeDtypeStruct((b, s, hp), BF16),
                   jax.ShapeDtypeStruct((b, s, hp), BF16), jax.ShapeDtypeStruct((b, s, CONV_CH), F32),
                   jax.ShapeDtypeStruct((b, s, SWA_HEADS * LANE), BF16),
                   jax.ShapeDtypeStruct((b, s, SWA_KV_HEADS * LANE), BF16),
                   jax.ShapeDtypeStruct((b, s, SWA_KV_HEADS * LANE), BF16)],
        compiler_params=_cparams(("arbitrary", "arbitrary")),
        name="in_projection",
    )(x, sc1.reshape(b, 1, d), sh1.reshape(b, 1, d), w_in_p, g_q.reshape(1, -1), w_q2,
      g_kv.reshape(1, -1), w_kv2, cos_t, sin_t)


def _mla_body(q_ref, k_ref, v_ref, o_ref, m_sc, l_sc, acc_sc, *, tq):
    qi = pl.program_id(2)
    q = q_ref[0]
    m_sc[...] = jnp.full(m_sc.shape, -jnp.inf, F32)
    l_sc[...] = jnp.zeros(l_sc.shape, F32)
    acc_sc[...] = jnp.zeros(acc_sc.shape, F32)

    def step(kstart, diagonal):
        k = k_ref[0, pl.ds(kstart, tq), :]
        v = v_ref[0, pl.ds(kstart, tq), :]
        s = lax.dot_general(q, k, _NT, preferred_element_type=F32)
        if diagonal:
            row = lax.broadcasted_iota(I32, s.shape, 0)
            col = lax.broadcasted_iota(I32, s.shape, 1)
            s = jnp.where(col <= row, s, -jnp.inf)
        m_prev = m_sc[...]
        m_new = jnp.maximum(m_prev, jnp.max(s, axis=1, keepdims=True))
        alpha = jnp.exp(m_prev - m_new)
        p = jnp.exp(s - m_new)
        l_sc[...] = alpha * l_sc[...] + jnp.sum(p, axis=1, keepdims=True)
        acc_sc[...] = alpha * acc_sc[...] + jnp.dot(p.astype(BF16), v, preferred_element_type=F32)
        m_sc[...] = m_new

    def loop_body(j, carry):
        step(pl.multiple_of(j * tq, tq), False)
        return carry

    lax.fori_loop(0, qi, loop_body, 0)
    step(pl.multiple_of(qi * tq, tq), True)
    o_ref[0] = (acc_sc[...] / l_sc[...]).astype(BF16)


def _mla_attention(q, k, v):
    b, s, hp = q.shape
    tq = TQ_ATT
    return pl.pallas_call(
        functools.partial(_mla_body, tq=tq),
        grid=(b, MLA_HEADS, s // tq),
        in_specs=[pl.BlockSpec((1, tq, LANE), lambda bi, h, i: (bi, i, h)),
                  pl.BlockSpec((1, s, LANE), lambda bi, h, i: (bi, 0, h)),
                  pl.BlockSpec((1, s, LANE), lambda bi, h, i: (bi, 0, h))],
        out_specs=pl.BlockSpec((1, tq, LANE), lambda bi, h, i: (bi, i, h)),
        out_shape=jax.ShapeDtypeStruct((b, s, hp), BF16),
        scratch_shapes=[pltpu.VMEM((tq, 1), F32), pltpu.VMEM((tq, 1), F32), pltpu.VMEM((tq, LANE), F32)],
        compiler_params=_cparams(("arbitrary", "arbitrary", "arbitrary")),
        name="mla_attention",
    )(q, k, v)


def _swa_body(sink_ref, q_ref, k_ref, kh_ref, v_ref, vh_ref, o_ref, *, nsub):
    i = pl.program_id(1)
    row = lax.broadcasted_iota(I32, (BLOCK, BLOCK), 0)
    col = lax.broadcasted_iota(I32, (BLOCK, BLOCK), 1)
    cur_ok = col <= row
    prev_band = col > row
    group = SWA_HEADS // SWA_KV_HEADS
    for j in range(nsub):
        rows = slice(j * BLOCK, (j + 1) * BLOCK)
        prev_ok = jnp.logical_and(prev_band, i > 0) if j == 0 else prev_band
        for g in range(SWA_KV_HEADS):
            lanes = slice(g * LANE, (g + 1) * LANE)
            if j == 0:
                kp, vp = kh_ref[0, :, lanes], vh_ref[0, :, lanes]
            else:
                prows = slice((j - 1) * BLOCK, j * BLOCK)
                kp, vp = k_ref[0, prows, lanes], v_ref[0, prows, lanes]
            kc, vc = k_ref[0, rows, lanes], v_ref[0, rows, lanes]
            for a in range(group):
                hq = g * group + a
                ql = slice(hq * LANE, (hq + 1) * LANE)
                qq = q_ref[0, rows, ql]
                sp = lax.dot_general(qq, kp, _NT, preferred_element_type=F32)
                sc = lax.dot_general(qq, kc, _NT, preferred_element_type=F32)
                sp = jnp.where(prev_ok, sp, -jnp.inf)
                sc = jnp.where(cur_ok, sc, -jnp.inf)
                sink = sink_ref[hq]
                m = jnp.maximum(jnp.maximum(jnp.max(sp, axis=1, keepdims=True),
                                            jnp.max(sc, axis=1, keepdims=True)), sink)
                pp = jnp.exp(sp - m)
                pc = jnp.exp(sc - m)
                den = (jnp.sum(pp, axis=1, keepdims=True) + jnp.sum(pc, axis=1, keepdims=True)
                       + jnp.exp(sink - m))
                o = (jnp.dot(pp.astype(BF16), vp, preferred_element_type=F32)
                     + jnp.dot(pc.astype(BF16), vc, preferred_element_type=F32)) / den
                o_ref[0, rows, ql] = o.astype(BF16)


def _swa_attention(qs, ks, vs, sinks):
    b, s, _ = qs.shape
    ts = TS_SWA
    nsub = ts // BLOCK
    halo = lambda w: pl.BlockSpec((1, BLOCK, w), lambda bi, i: (bi, jnp.maximum(i * nsub - 1, 0), 0))
    tok = lambda w: pl.BlockSpec((1, ts, w), lambda bi, i: (bi, i, 0))
    wq, wk = SWA_HEADS * LANE, SWA_KV_HEADS * LANE
    return pl.pallas_call(
        functools.partial(_swa_body, nsub=nsub),
        grid=(b, s // ts),
        in_specs=[pl.BlockSpec(memory_space=pltpu.SMEM), tok(wq), tok(wk), halo(wk), tok(wk), halo(wk)],
        out_specs=tok(wq),
        out_shape=jax.ShapeDtypeStruct((b, s, wq), BF16),
        compiler_params=_cparams(("arbitrary", "arbitrary")),
        name="swa_attention",
    )(sinks, qs, ks, ks, vs, vs)


def _conv_body(g_ref, gh_ref, wdw_ref, bdw_ref, gln_ref, bln_ref, wpw_ref, bpw_ref, o_ref, buf, *, ts):
    i = pl.program_id(1)
    halo = gh_ref[0]
    buf[0:CONV_HALO, :] = jnp.where(i > 0, halo, jnp.zeros_like(halo))
    buf[CONV_HALO:CONV_HALO + ts, :] = g_ref[0]
    first = CONV_HALO - (CONV_WIDTH - 1)
    acc = jnp.zeros((ts, CONV_CH), F32) + bdw_ref[...]
    for j in range(CONV_WIDTH):
        acc = acc + wdw_ref[j:j + 1, :] * buf[first + j:first + j + ts, :]
    hn = _layer_norm(acc, gln_ref[...], bln_ref[...])
    hn = hn * jax.nn.sigmoid(hn)
    o = jnp.dot(hn.astype(BF16), wpw_ref[...], preferred_element_type=F32) + bpw_ref[...]
    o_ref[0] = o.astype(BF16)


def _conformer_conv(g, w_dw, b_dw, g_ln, b_ln, w_pw, b_pw):
    b, s, ch = g.shape
    ts = TS_CONV
    per = ts // CONV_HALO
    full = lambda shape: pl.BlockSpec(shape, lambda bi, i: (0,) * len(shape))
    return pl.pallas_call(
        functools.partial(_conv_body, ts=ts),
        grid=(b, s // ts),
        in_specs=[pl.BlockSpec((1, ts, ch), lambda bi, i: (bi, i, 0)),
                  pl.BlockSpec((1, CONV_HALO, ch), lambda bi, i: (bi, jnp.maximum(i * per - 1, 0), 0)),
                  full((CONV_WIDTH, ch)), full((1, ch)), full((1, ch)), full((1, ch)),
                  full((ch, ch)), full((1, ch))],
        out_specs=pl.BlockSpec((1, ts, ch), lambda bi, i: (bi, i, 0)),
        out_shape=jax.ShapeDtypeStruct((b, s, ch), BF16),
        scratch_shapes=[pltpu.VMEM((CONV_HALO + ts, ch), F32)],
        compiler_params=_cparams(("arbitrary", "arbitrary")),
        name="conformer_conv",
    )(g, g, w_dw, b_dw.reshape(1, ch), g_ln.reshape(1, ch), b_ln.reshape(1, ch),
      w_pw.astype(BF16), b_pw.reshape(1, ch))


def _outproj_body(om_ref, oc_ref, os_ref, x_ref, gt_ref, wa_ref, wb_ref, wc_ref, g_ref, b_ref, o_ref, *, alpha):
    mix = (jnp.dot(om_ref[0], wa_ref[...], preferred_element_type=F32)
           + jnp.dot(oc_ref[0], wb_ref[...], preferred_element_type=F32)
           + jnp.dot(os_ref[0], wc_ref[...], preferred_element_type=F32))
    y = alpha * x_ref[0] + (1.0 + gt_ref[0]) * mix
    o_ref[0] = _layer_norm(y, g_ref[...], b_ref[...])


def _prep_outproj_weights(w_o):
    d = w_o.shape[1]
    mla_out = MLA_HEADS * MLA_V
    swa_out = SWA_HEADS * SWA_HEAD_DIM
    wa = w_o[:mla_out].reshape(MLA_HEADS, MLA_V, d)
    wa = jnp.pad(wa, ((0, 0), (0, LANE - MLA_V), (0, 0))).reshape(MLA_HEADS * LANE, d)
    wb = w_o[mla_out:mla_out + CONV_CH]
    wc = w_o[mla_out + CONV_CH:].reshape(SWA_HEADS, SWA_HEAD_DIM, d)
    wc = jnp.pad(wc, ((0, 0), (0, LANE - SWA_HEAD_DIM), (0, 0))).reshape(SWA_HEADS * LANE, d)
    assert wc.shape[0] == SWA_HEADS * LANE and swa_out == SWA_HEADS * SWA_HEAD_DIM
    return wa.astype(BF16), wb.astype(BF16), wc.astype(BF16)


def _out_projection(o_mla, o_conv, o_swa, x, gt1, wa, wb, wc, g_ln, b_ln, alpha):
    b, s, d = x.shape
    tq = TQ_PROJ
    full = lambda shape: pl.BlockSpec(shape, lambda bi, i: (0,) * len(shape))
    tok = lambda w: pl.BlockSpec((1, tq, w), lambda bi, i: (bi, i, 0))
    return pl.pallas_call(
        functools.partial(_outproj_body, alpha=alpha),
        grid=(b, s // tq),
        in_specs=[tok(o_mla.shape[2]), tok(o_conv.shape[2]), tok(o_swa.shape[2]), tok(d),
                  pl.BlockSpec((1, 1, d), lambda bi, i: (bi, 0, 0)),
                  full(wa.shape), full(wb.shape), full(wc.shape), full((1, d)), full((1, d))],
        out_specs=tok(d),
        out_shape=jax.ShapeDtypeStruct((b, s, d), F32),
        compiler_params=_cparams(("arbitrary", "arbitrary")),
        name="out_projection",
    )(o_mla, o_conv, o_swa, x, gt1.reshape(b, 1, d), wa, wb, wc, g_ln.reshape(1, d), b_ln.reshape(1, d))


def _topk_rows(s, nrows, k, val_sc, idx_sc, payload=None, pay_sc=None):
    rid = lax.broadcasted_iota(I32, s.shape, 0)
    for r in range(k):
        m = jnp.max(s, axis=0, keepdims=True)
        sel = jnp.min(jnp.where(s == m, rid, nrows), axis=0, keepdims=True)
        hit = rid == sel
        val_sc[r:r + 1, :] = m
        if payload is None:
            idx_sc[r:r + 1, :] = sel
        else:
            pay_sc[r:r + 1, :] = jnp.max(jnp.where(hit, payload, -1), axis=0, keepdims=True)
        s = jnp.where(hit, -jnp.inf, s)


def _route_body(x_ref, sc_ref, sh_ref, wpq_ref, keys_ref, idx_ref, gate_ref,
                s1_sc, i1_sc, s2_sc, i2_sc, cand_sc, cidx_sc, top_sc, e_sc, idxT_sc, gateT_sc):
    k = PEER_TOPK
    h = x_ref[0] * (1.0 + sc_ref[0]) + sh_ref[0]
    qp = jnp.dot(h.astype(BF16), wpq_ref[...], preferred_element_type=F32)
    for hd in range(PEER_HEADS):
        for half, (v_sc, i_sc) in enumerate(((s1_sc, i1_sc), (s2_sc, i2_sc))):
            col = (hd * 2 + half) * LANE
            qh = qp[:, col:col + LANE].astype(BF16)
            sc = lax.dot_general(keys_ref[hd * 2 + half], qh, _NT, preferred_element_type=F32)
            _topk_rows(sc, N_SUBKEYS, k, v_sc, i_sc)
        s2 = s2_sc[...]
        i2 = i2_sc[...]
        for a in range(k):
            cand_sc[a * k:(a + 1) * k, :] = s1_sc[a:a + 1, :] + s2
            cidx_sc[a * k:(a + 1) * k, :] = i1_sc[a:a + 1, :] * N_SUBKEYS + i2
        _topk_rows(cand_sc[...], k * k, k, top_sc, None, payload=cidx_sc[...], pay_sc=e_sc)
        top = top_sc[...]
        ex = jnp.exp(top - top[0:1, :])
        gateT_sc[hd * k:(hd + 1) * k, :] = ex / jnp.sum(ex, axis=0, keepdims=True)
        idxT_sc[hd * k:(hd + 1) * k, :] = e_sc[...] * ROWS_PER_EXPERT
    idx_ref[...] = idxT_sc[...].T
    gate_ref[...] = gateT_sc[...].T


def _peer_route(x1, sc2, sh2, w_pq, keys):
    b, s, d = x1.shape
    tt = TT_ROUTE
    k = PEER_TOPK
    nq = w_pq.shape[1]
    per = s // tt
    full = lambda shape: pl.BlockSpec(shape, lambda bi, i: (0,) * len(shape))
    vec = pl.BlockSpec((1, 1, d), lambda bi, i: (bi, 0, 0))
    out = pl.BlockSpec((tt, PEER_HK), lambda bi, i: (bi * per + i, 0))
    return pl.pallas_call(
        _route_body,
        grid=(b, per),
        in_specs=[pl.BlockSpec((1, tt, d), lambda bi, i: (bi, i, 0)), vec, vec, full((d, nq)),
                  full(keys.shape)],
        out_specs=[out, out],
        out_shape=[jax.ShapeDtypeStruct((b * s, PEER_HK), I32), jax.ShapeDtypeStruct((b * s, PEER_HK), F32)],
        scratch_shapes=[pltpu.VMEM((k, tt), F32), pltpu.VMEM((k, tt), I32),
                        pltpu.VMEM((k, tt), F32), pltpu.VMEM((k, tt), I32),
                        pltpu.VMEM((k * k, tt), F32), pltpu.VMEM((k * k, tt), I32),
                        pltpu.VMEM((k, tt), F32), pltpu.VMEM((k, tt), I32),
                        pltpu.VMEM((PEER_HK, tt), I32), pltpu.VMEM((PEER_HK, tt), F32)],
        compiler_params=_cparams(("arbitrary", "arbitrary")),
        name="peer_route",
    )(x1, sc2.reshape(b, 1, d), sh2.reshape(b, 1, d), w_pq, keys)


def _pack_table(tab):
    n, d = tab.shape
    t16 = lax.bitcast_convert_type(tab.astype(BF16), jnp.uint16).astype(jnp.uint32)
    words = (t16[:, :d // 2] << 16) | t16[:, d // 2:]
    return lax.bitcast_convert_type(words, I32).reshape(n * ROWS_PER_EXPERT, LANE)


def _unpack(words):
    hi = lax.bitcast_convert_type(words & jnp.int32(-65536), F32)
    lo = lax.bitcast_convert_type(words << 16, F32)
    return hi, lo


def _load_table(tab_hbm, tab_vmem, sem):
    @pl.when(pl.program_id(0) == 0)
    def _():
        cp = pltpu.make_async_copy(tab_hbm, tab_vmem, sem)
        cp.start()
        cp.wait()


def _gather_rows(idx_ref, t, tab_vmem, gbuf):
    base = t * PEER_HK
    for j in range(PEER_HK):
        r = pl.multiple_of(idx_ref[base + j], ROWS_PER_EXPERT)
        gbuf[j * ROWS_PER_EXPERT:(j + 1) * ROWS_PER_EXPERT, :] = tab_vmem[pl.ds(r, ROWS_PER_EXPERT), :]


def _peer_u_body(idx_ref, x_ref, sc_ref, sh_ref, gate_ref, tab_hbm, w_ref, tab_vmem, gbuf, sem, *, tn):
    _load_table(tab_hbm, tab_vmem, sem)
    scale = 1.0 + sc_ref[0]
    shift = sh_ref[0]
    ones = jnp.ones((SUBLANE, LANE), F32)

    def tok(t, carry):
        _gather_rows(idx_ref, t, tab_vmem, gbuf)
        h = x_ref[pl.ds(pl.multiple_of(t * SUBLANE, SUBLANE), SUBLANE), :] * scale + shift
        acc = jnp.zeros((PEER_HK, LANE), F32)
        for c in range(ROWS_PER_EXPERT):
            hi, lo = _unpack(gbuf[pl.ds(c, PEER_HK, stride=ROWS_PER_EXPERT), :])
            acc = acc + hi * h[c:c + 1, :] + lo * h[ROWS_PER_EXPERT + c:ROWS_PER_EXPERT + c + 1, :]
        srow = lax.dot_general(ones, acc, _NT, preferred_element_type=F32,
                               precision=lax.Precision.HIGHEST)[0:1, :]
        act = 0.5 * srow * (1.0 + lax.erf(srow * (2.0 ** -0.5)))
        w_ref[pl.ds(t, 1), :] = gate_ref[pl.ds(t, 1), :] * act
        return carry

    lax.fori_loop(0, tn, tok, 0)


def _peer_v_body(idx_ref, w_ref, tab_hbm, y_ref, tab_vmem, gbuf, sem, *, tn):
    _load_table(tab_hbm, tab_vmem, sem)

    def tok(t, carry):
        _gather_rows(idx_ref, t, tab_vmem, gbuf)
        wrow = w_ref[pl.ds(t, 1), :]
        wrep = jnp.broadcast_to(wrow, (LANE, PEER_HK)).T
        his, los = [], []
        for c in range(ROWS_PER_EXPERT):
            hi, lo = _unpack(gbuf[pl.ds(c, PEER_HK, stride=ROWS_PER_EXPERT), :])
            his.append(jnp.sum(wrep * hi, axis=0, keepdims=True))
            los.append(jnp.sum(wrep * lo, axis=0, keepdims=True))
        y_ref[pl.ds(pl.multiple_of(t * SUBLANE, SUBLANE), SUBLANE), :] = jnp.concatenate(his + los, axis=0)
        return carry

    lax.fori_loop(0, tn, tok, 0)


def _peer_experts(idx, gate, x1, sc2, sh2, u_pack, v_pack):
    b, s, d = x1.shape
    t = b * s
    tn = TN_PEER
    per = s // tn
    rows = d // LANE
    assert rows == SUBLANE and d == 2 * ROWS_PER_EXPERT * LANE
    idx_flat = idx.reshape(t * PEER_HK)
    idx_spec = pl.BlockSpec((tn * PEER_HK,), lambda i: (i,), memory_space=pltpu.SMEM)
    tokrows = pl.BlockSpec((tn * rows, LANE), lambda i: (i, 0))
    tokhk = pl.BlockSpec((tn, PEER_HK), lambda i: (i, 0))
    vec = pl.BlockSpec((1, rows, LANE), lambda i: (i // per, 0, 0))
    table = pl.BlockSpec(memory_space=pl.ANY)
    scratch = [pltpu.VMEM(u_pack.shape, I32), pltpu.VMEM((PEER_HK * ROWS_PER_EXPERT, LANE), I32),
               pltpu.SemaphoreType.DMA(())]
    w = pl.pallas_call(
        functools.partial(_peer_u_body, tn=tn),
        grid=(t // tn,),
        in_specs=[idx_spec, tokrows, vec, vec, tokhk, table],
        out_specs=tokhk,
        out_shape=jax.ShapeDtypeStruct((t, PEER_HK), F32),
        scratch_shapes=scratch,
        compiler_params=_cparams(("arbitrary",), VMEM_LIMIT_TABLE),
        name="peer_expert_in",
    )(idx_flat, x1.reshape(t * rows, LANE), sc2.reshape(b, rows, LANE), sh2.reshape(b, rows, LANE), gate, u_pack)
    y = pl.pallas_call(
        functools.partial(_peer_v_body, tn=tn),
        grid=(t // tn,),
        in_specs=[idx_spec, tokhk, table],
        out_specs=tokrows,
        out_shape=jax.ShapeDtypeStruct((t * rows, LANE), F32),
        scratch_shapes=scratch,
        compiler_params=_cparams(("arbitrary",), VMEM_LIMIT_TABLE),
        name="peer_expert_out",
    )(idx_flat, w, v_pack)
    return y.reshape(b, s, d)


def _resnorm_body(x_ref, y_ref, gt_ref, g_ref, b_ref, o_ref, *, alpha):
    v = alpha * x_ref[0] + (1.0 + gt_ref[0]) * y_ref[0]
    o_ref[0] = _layer_norm(v, g_ref[...], b_ref[...])


def _residual_norm(x, y, gt, g_ln, b_ln, alpha):
    b, s, d = x.shape
    tq = TQ_PROJ
    tok = pl.BlockSpec((1, tq, d), lambda bi, i: (bi, i, 0))
    full = pl.BlockSpec((1, d), lambda bi, i: (0, 0))
    return pl.pallas_call(
        functools.partial(_resnorm_body, alpha=alpha),
        grid=(b, s // tq),
        in_specs=[tok, tok, pl.BlockSpec((1, 1, d), lambda bi, i: (bi, 0, 0)), full, full],
        out_specs=tok,
        out_shape=jax.ShapeDtypeStruct((b, s, d), F32),
        compiler_params=_cparams(("arbitrary", "arbitrary")),
        name="residual_norm",
    )(x, y, gt.reshape(b, 1, d), g_ln.reshape(1, d), b_ln.reshape(1, d))


def kernel(x, c, w_ada, b_ada, w_in, g_q, w_uq, g_kv, w_ukv, w_dw, b_dw, g_conv, b_conv, w_pw, b_pw, sinks,
           w_o, g_ln1, b_ln1, w_pq, sub_keys, u_tab, v_tab, g_ln2, b_ln2):
    depth = w_ada.shape[0]
    b, s, d = x.shape
    alpha = float((2 * depth) ** 0.25)
    cos_t, sin_t = _rope_tables(s)
    mod = _ada_mod(c, w_ada, b_ada)
    for l in range(depth):
        sh1, sc1, gt1, sh2, sc2, gt2 = [mod[l, :, i * d:(i + 1) * d] for i in range(6)]
        w_in_p, w_q2, w_kv2 = _prep_inproj_weights(w_in[l], w_uq[l], w_ukv[l])
        q, k, v, g, qs, ks, vs = _in_projection(x, sc1, sh1, w_in_p, g_q[l], w_q2, g_kv[l], w_kv2, cos_t, sin_t)
        o_mla = _mla_attention(q, k, v)
        o_swa = _swa_attention(qs, ks, vs, sinks[l])
        o_conv = _conformer_conv(g, w_dw[l], b_dw[l], g_conv[l], b_conv[l], w_pw[l], b_pw[l])
        wa, wb, wc = _prep_outproj_weights(w_o[l])
        x = _out_projection(o_mla, o_conv, o_swa, x, gt1, wa, wb, wc, g_ln1[l], b_ln1[l], alpha)

        keys = sub_keys[l].reshape(PEER_HEADS * 2, N_SUBKEYS, -1).astype(BF16)
        idx, gate = _peer_route(x, sc2, sh2, w_pq[l].astype(BF16), keys)
        y = _peer_experts(idx, gate, x, sc2, sh2, _pack_table(u_tab[l]), _pack_table(v_tab[l]))
        x = _residual_norm(x, y, gt2, g_ln2[l], b_ln2[l], alpha)
    return x
```

```python
import functools

import jax
import jax.numpy as jnp
from jax import lax
from jax.experimental import pallas as pl
from jax.experimental.pallas import tpu as pltpu

F32 = jnp.float32
BF16 = jnp.bfloat16
I32 = jnp.int32

LANE = 128
SUBLANE = 8
VMEM_BYTES_V7X = 64 * 1024 * 1024
VMEM_LIMIT = 48 * 1024 * 1024
VMEM_LIMIT_TABLE = 56 * 1024 * 1024

MLA_HEADS = 8
MLA_NOPE = 64
MLA_ROPE = 32
MLA_V = 64
MLA_Q_RANK = 256
MLA_KV_RANK = 128
ROPE_BASE = 10000.0
CONV_CH = 256
CONV_WIDTH = 31
SWA_HEADS = 4
SWA_KV_HEADS = 2
SWA_HEAD_DIM = 64
BLOCK = 128
IN_SPLITS = (MLA_Q_RANK, MLA_KV_RANK, MLA_ROPE, 2 * CONV_CH,
             SWA_HEADS * SWA_HEAD_DIM, SWA_KV_HEADS * SWA_HEAD_DIM, SWA_KV_HEADS * SWA_HEAD_DIM)
N_SUBKEYS = 128
PEER_HEADS = 8
PEER_TOPK = 16
PEER_HK = PEER_HEADS * PEER_TOPK
EPS = 1e-6

TQ_PROJ = 512
TQ_ATT = 512
MLA_ROW_SPLIT = 2
TS_SWA = 512
TS_CONV = 512
CONV_HALO = 32
TT_ROUTE = 256
TN_PEER = 256
ROWS_PER_EXPERT = 4

_NT = (((1,), (1,)), ((), ()))


def _cparams(sem, limit=VMEM_LIMIT):
    return pltpu.CompilerParams(dimension_semantics=sem, vmem_limit_bytes=limit)


def _layer_norm(v, g, b):
    mu = jnp.mean(v, axis=-1, keepdims=True)
    d = v - mu
    var = jnp.mean(d * d, axis=-1, keepdims=True)
    return d * lax.rsqrt(var + EPS) * g + b


def _rms_norm(v, g):
    return v * lax.rsqrt(jnp.mean(v * v, axis=-1, keepdims=True) + EPS) * g


def _ada_body(c_ref, w_ref, b_ref, o_ref):
    c = c_ref[...]
    cs = c * jax.nn.sigmoid(c)
    o_ref[0] = jnp.dot(cs, w_ref[0], preferred_element_type=F32,
                       precision=lax.Precision.HIGHEST) + b_ref[0]


def _ada_mod(c, w_ada, b_ada):
    depth, d, d6 = w_ada.shape
    bp = -(-c.shape[0] // SUBLANE) * SUBLANE
    cp = jnp.pad(c, ((0, bp - c.shape[0]), (0, 0)))
    tn = d
    out = pl.pallas_call(
        _ada_body,
        grid=(depth, d6 // tn),
        in_specs=[pl.BlockSpec((bp, d), lambda l, j: (0, 0)),
                  pl.BlockSpec((1, d, tn), lambda l, j: (l, 0, j)),
                  pl.BlockSpec((1, 1, tn), lambda l, j: (l, 0, j))],
        out_specs=pl.BlockSpec((1, bp, tn), lambda l, j: (l, 0, j)),
        out_shape=jax.ShapeDtypeStruct((depth, bp, d6), F32),
        compiler_params=_cparams(("arbitrary", "arbitrary")),
        name="ada_mod",
    )(cp, w_ada, b_ada.reshape(depth, 1, d6))
    return out[:, :c.shape[0]]


def _inproj_body(x_ref, sc_ref, sh_ref, win_ref, gq_ref, wq_ref, gkv_ref, wkv_ref, cos_ref, sin_ref,
                 q_ref, k_ref, v_ref, g_ref, qs_ref, ks_ref, vs_ref, *, mla_scale, swa_scale):
    hp = MLA_HEADS * LANE
    h = x_ref[0] * (1.0 + sc_ref[0]) + sh_ref[0]
    proj = jnp.dot(h.astype(BF16), win_ref[...], preferred_element_type=F32)
    o = 0
    c_q = proj[:, o:o + MLA_Q_RANK]; o += MLA_Q_RANK
    c_kv = proj[:, o:o + MLA_KV_RANK]; o += MLA_KV_RANK
    kr = proj[:, o:o + LANE]; o += LANE
    kr_sw = proj[:, o:o + LANE]; o += LANE
    a = proj[:, o:o + CONV_CH]; o += CONV_CH
    gate = proj[:, o:o + CONV_CH]; o += CONV_CH
    qs = proj[:, o:o + SWA_HEADS * LANE]; o += SWA_HEADS * LANE
    ks = proj[:, o:o + SWA_KV_HEADS * LANE]; o += SWA_KV_HEADS * LANE
    vs = proj[:, o:o + SWA_KV_HEADS * LANE]

    cos = cos_ref[...]
    sin = sin_ref[...]
    nq = _rms_norm(c_q, gq_ref[...]).astype(BF16)
    q2 = jnp.dot(nq, wq_ref[...], preferred_element_type=F32)
    nkv = _rms_norm(c_kv, gkv_ref[...]).astype(BF16)
    kv2 = jnp.dot(nkv, wkv_ref[...], preferred_element_type=F32)
    k_rope = kr * cos + kr_sw * sin
    for hd in range(MLA_HEADS):
        sl = slice(hd * LANE, (hd + 1) * LANE)
        sl2 = slice(hp + hd * LANE, hp + (hd + 1) * LANE)
        q_ref[0, :, sl] = ((q2[:, sl] * cos + q2[:, sl2] * sin) * mla_scale).astype(BF16)
        k_ref[0, :, sl] = (kv2[:, sl] + k_rope).astype(BF16)
    v_ref[0] = kv2[:, hp:].astype(BF16)
    g_ref[0] = a * jax.nn.sigmoid(gate)
    qs_ref[0] = (qs * swa_scale).astype(BF16)
    ks_ref[0] = ks.astype(BF16)
    vs_ref[0] = vs.astype(BF16)


def _pad_heads(w, heads, width):
    r = w.shape[0]
    w = w.reshape(r, heads, width)
    return jnp.pad(w, ((0, 0), (0, 0), (0, LANE - width))).reshape(r, heads * LANE)


def _rope_place(r1, r2):
    z64 = jnp.zeros(r1.shape[:-1] + (MLA_NOPE,), r1.dtype)
    z32 = jnp.zeros(r1.shape[:-1] + (LANE - MLA_NOPE - MLA_ROPE,), r1.dtype)
    return jnp.concatenate([z64, r1, r2, z32], axis=-1)


def _prep_inproj_weights(w_in, w_uq, w_ukv):
    d = w_in.shape[0]
    half = MLA_ROPE // 2
    offs = [0]
    for s in IN_SPLITS:
        offs.append(offs[-1] + s)
    c_q, c_kv, k_r, conv, q_s, k_s, v_s = [w_in[:, offs[i]:offs[i + 1]] for i in range(7)]
    kr1, kr2 = k_r[:, :half], k_r[:, half:]
    w_in_p = jnp.concatenate([
        c_q, c_kv, _rope_place(kr1, kr2), _rope_place(-kr2, kr1), conv,
        _pad_heads(q_s, SWA_HEADS, SWA_HEAD_DIM), _pad_heads(k_s, SWA_KV_HEADS, SWA_HEAD_DIM),
        _pad_heads(v_s, SWA_KV_HEADS, SWA_HEAD_DIM)], axis=1).astype(BF16)

    wq = w_uq.reshape(MLA_Q_RANK, MLA_HEADS, MLA_NOPE + MLA_ROPE)
    nope, r1, r2 = wq[..., :MLA_NOPE], wq[..., MLA_NOPE:MLA_NOPE + half], wq[..., MLA_NOPE + half:]
    zpad = jnp.zeros(nope.shape[:-1] + (LANE - MLA_NOPE - MLA_ROPE,), nope.dtype)
    placed = jnp.concatenate([nope, r1, r2, zpad], axis=-1).reshape(MLA_Q_RANK, MLA_HEADS * LANE)
    partner = _rope_place(-r2, r1).reshape(MLA_Q_RANK, MLA_HEADS * LANE)
    w_q2 = jnp.concatenate([placed, partner], axis=1).astype(BF16)

    wkv = w_ukv.reshape(MLA_KV_RANK, MLA_HEADS, MLA_NOPE + MLA_V)
    kn = jnp.pad(wkv[..., :MLA_NOPE], ((0, 0), (0, 0), (0, LANE - MLA_NOPE)))
    vv = jnp.pad(wkv[..., MLA_NOPE:], ((0, 0), (0, 0), (0, LANE - MLA_V)))
    w_kv2 = jnp.concatenate([kn.reshape(MLA_KV_RANK, -1), vv.reshape(MLA_KV_RANK, -1)], axis=1).astype(BF16)
    return w_in_p, w_q2, w_kv2


def _rope_tables(seq):
    half = MLA_ROPE // 2
    inv = 1.0 / (ROPE_BASE ** (jnp.arange(0, MLA_ROPE, 2, dtype=F32) / MLA_ROPE))
    ang = jnp.arange(seq, dtype=F32)[:, None] * inv[None, :]
    cos, sin = jnp.cos(ang), jnp.sin(ang)
    ones = jnp.ones((seq, MLA_NOPE), F32)
    zpad = jnp.zeros((seq, LANE - MLA_NOPE - MLA_ROPE), F32)
    cos_t = jnp.concatenate([ones, cos, cos, zpad], axis=1)
    sin_t = jnp.concatenate([jnp.zeros((seq, MLA_NOPE), F32), sin, sin, zpad], axis=1)
    assert half * 2 == MLA_ROPE
    return cos_t, sin_t


def _in_projection(x, sc1, sh1, w_in_p, g_q, w_q2, g_kv, w_kv2, cos_t, sin_t):
    b, s, d = x.shape
    tq = TQ_PROJ
    ncol = w_in_p.shape[1]
    hp = MLA_HEADS * LANE
    full = lambda shape: pl.BlockSpec(shape, lambda bi, i: (0,) * len(shape))
    tok = lambda w: pl.BlockSpec((1, tq, w), lambda bi, i: (bi, i, 0))
    vec = pl.BlockSpec((1, 1, d), lambda bi, i: (bi, 0, 0))
    body = functools.partial(_inproj_body, mla_scale=float((MLA_NOPE + MLA_ROPE) ** -0.5),
                             swa_scale=float(SWA_HEAD_DIM ** -0.5))
    return pl.pallas_call(
        body,
        grid=(b, s // tq),
        in_specs=[tok(d), vec, vec, full((d, ncol)), full((1, MLA_Q_RANK)), full((MLA_Q_RANK, 2 * hp)),
                  full((1, MLA_KV_RANK)), full((MLA_KV_RANK, 2 * hp)),
                  pl.BlockSpec((tq, LANE), lambda bi, i: (i, 0)),
                  pl.BlockSpec((tq, LANE), lambda bi, i: (i, 0))],
        out_specs=[tok(hp), tok(hp), tok(hp), tok(CONV_CH), tok(SWA_HEADS * LANE),
                   tok(SWA_KV_HEADS * LANE), tok(SWA_KV_HEADS * LANE)],
        out_shape=[jax.ShapeDtypeStruct((b, s, hp), BF16), jax.ShapeDtypeStruct((b, s, hp), BF16),
                   jax.ShapeDtypeStruct((b, s, hp), BF16), jax.ShapeDtypeStruct((b, s, CONV_CH), F32),
                   jax.ShapeDtypeStruct((b, s, SWA_HEADS * LANE), BF16),
                   jax.ShapeDtypeStruct((b, s, SWA_KV_HEADS * LANE), BF16),
                   jax.ShapeDtypeStruct((b, s, SWA_KV_HEADS * LANE), BF16)],
        compiler_params=_cparams(("arbitrary", "arbitrary")),
        name="in_projection",
    )(x, sc1.reshape(b, 1, d), sh1.reshape(b, 1, d), w_in_p, g_q.reshape(1, -1), w_q2,
      g_kv.reshape(1, -1), w_kv2, cos_t, sin_t)


def _fold_lane_tiles(v, op):
    out = v[:, 0:LANE]
    for c in range(1, v.shape[1] // LANE):
        out = op(out, v[:, c * LANE:(c + 1) * LANE])
    return out


def _mla_body(q_ref, k_ref, v_ref, o_ref, m_sc, l_sc, acc_sc, *, tq):
    qi = pl.program_id(2)
    m_sc[...] = jnp.full(m_sc.shape, -jnp.inf, F32)
    l_sc[...] = jnp.zeros(l_sc.shape, F32)
    acc_sc[...] = jnp.zeros(acc_sc.shape, F32)
    tr = tq // MLA_ROW_SPLIT

    def scores(j):
        k = k_ref[0, pl.ds(pl.multiple_of(j * tq, tq), tq), :]
        return lax.dot_general(q_ref[0], k, _NT, preferred_element_type=F32)

    def accumulate(s_all, j, diagonal):
        v = v_ref[0, pl.ds(pl.multiple_of(j * tq, tq), tq), :]
        for r in range(MLA_ROW_SPLIT):
            rows = slice(r * tr, (r + 1) * tr)
            s = s_all[rows, :]
            if diagonal:
                row = lax.broadcasted_iota(I32, s.shape, 0) + r * tr
                col = lax.broadcasted_iota(I32, s.shape, 1)
                s = jnp.where(col <= row, s, -jnp.inf)
            m_prev = m_sc[rows, :]
            m_new = jnp.maximum(m_prev, jnp.max(_fold_lane_tiles(s, jnp.maximum), axis=1, keepdims=True))
            alpha = jnp.exp(m_prev - m_new)
            p = jnp.exp(s - jnp.concatenate([m_new] * (tq // LANE), axis=1))
            l_sc[rows, :] = alpha * l_sc[rows, :] + _fold_lane_tiles(p, jnp.add)
            acc_sc[rows, :] = alpha * acc_sc[rows, :] + jnp.dot(p.astype(BF16), v, preferred_element_type=F32)
            m_sc[rows, :] = m_new

    def loop_body(j, s_cur):
        s_next = scores(j + 1)
        accumulate(s_cur, j, False)
        return s_next

    s_last = lax.fori_loop(0, qi, loop_body, scores(0))
    accumulate(s_last, qi, True)
    o_ref[0] = (acc_sc[...] / jnp.sum(l_sc[...], axis=1, keepdims=True)).astype(BF16)


def _mla_attention(q, k, v):
    b, s, hp = q.shape
    tq = TQ_ATT
    return pl.pallas_call(
        functools.partial(_mla_body, tq=tq),
        grid=(b, MLA_HEADS, s // tq),
        in_specs=[pl.BlockSpec((1, tq, LANE), lambda bi, h, i: (bi, i, h)),
                  pl.BlockSpec((1, s, LANE), lambda bi, h, i: (bi, 0, h)),
                  pl.BlockSpec((1, s, LANE), lambda bi, h, i: (bi, 0, h))],
        out_specs=pl.BlockSpec((1, tq, LANE), lambda bi, h, i: (bi, i, h)),
        out_shape=jax.ShapeDtypeStruct((b, s, hp), BF16),
        scratch_shapes=[pltpu.VMEM((tq, LANE), F32), pltpu.VMEM((tq, LANE), F32), pltpu.VMEM((tq, LANE), F32)],
        compiler_params=_cparams(("arbitrary", "arbitrary", "arbitrary")),
        name="mla_attention",
    )(q, k, v)


def _swa_body(sink_ref, q_ref, k_ref, kh_ref, v_ref, vh_ref, o_ref, *, nsub):
    i = pl.program_id(1)
    row = lax.broadcasted_iota(I32, (BLOCK, BLOCK), 0)
    col = lax.broadcasted_iota(I32, (BLOCK, BLOCK), 1)
    cur_ok = col <= row
    prev_band = col > row
    group = SWA_HEADS // SWA_KV_HEADS
    for j in range(nsub):
        rows = slice(j * BLOCK, (j + 1) * BLOCK)
        prev_ok = jnp.logical_and(prev_band, i > 0) if j == 0 else prev_band
        for g in range(SWA_KV_HEADS):
            lanes = slice(g * LANE, (g + 1) * LANE)
            if j == 0:
                kp, vp = kh_ref[0, :, lanes], vh_ref[0, :, lanes]
            else:
                prows = slice((j - 1) * BLOCK, j * BLOCK)
                kp, vp = k_ref[0, prows, lanes], v_ref[0, prows, lanes]
            kc, vc = k_ref[0, rows, lanes], v_ref[0, rows, lanes]
            for a in range(group):
                hq = g * group + a
                ql = slice(hq * LANE, (hq + 1) * LANE)
                qq = q_ref[0, rows, ql]
                sp = lax.dot_general(qq, kp, _NT, preferred_element_type=F32)
                sc = lax.dot_general(qq, kc, _NT, preferred_element_type=F32)
                sp = jnp.where(prev_ok, sp, -jnp.inf)
                sc = jnp.where(cur_ok, sc, -jnp.inf)
                sink = sink_ref[hq]
                m = jnp.maximum(jnp.maximum(jnp.max(sp, axis=1, keepdims=True),
                                            jnp.max(sc, axis=1, keepdims=True)), sink)
                pp = jnp.exp(sp - m)
                pc = jnp.exp(sc - m)
                den = (jnp.sum(pp, axis=1, keepdims=True) + jnp.sum(pc, axis=1, keepdims=True)
                       + jnp.exp(sink - m))
                o = (jnp.dot(pp.astype(BF16), vp, preferred_element_type=F32)
                     + jnp.dot(pc.astype(BF16), vc, preferred_element_type=F32)) / den
                o_ref[0, rows, ql] = o.astype(BF16)


def _swa_attention(qs, ks, vs, sinks):
    b, s, _ = qs.shape
    ts = TS_SWA
    nsub = ts // BLOCK
    halo = lambda w: pl.BlockSpec((1, BLOCK, w), lambda bi, i: (bi, jnp.maximum(i * nsub - 1, 0), 0))
    tok = lambda w: pl.BlockSpec((1, ts, w), lambda bi, i: (bi, i, 0))
    wq, wk = SWA_HEADS * LANE, SWA_KV_HEADS * LANE
    return pl.pallas_call(
        functools.partial(_swa_body, nsub=nsub),
        grid=(b, s // ts),
        in_specs=[pl.BlockSpec(memory_space=pltpu.SMEM), tok(wq), tok(wk), halo(wk), tok(wk), halo(wk)],
        out_specs=tok(wq),
        out_shape=jax.ShapeDtypeStruct((b, s, wq), BF16),
        compiler_params=_cparams(("arbitrary", "arbitrary")),
        name="swa_attention",
    )(sinks, qs, ks, ks, vs, vs)


def _conv_body(g_ref, gh_ref, wdw_ref, bdw_ref, gln_ref, bln_ref, wpw_ref, bpw_ref, o_ref, buf, *, ts):
    i = pl.program_id(1)
    halo = gh_ref[0]
    buf[0:CONV_HALO, :] = jnp.where(i > 0, halo, jnp.zeros_like(halo))
    buf[CONV_HALO:CONV_HALO + ts, :] = g_ref[0]
    first = CONV_HALO - (CONV_WIDTH - 1)
    acc = jnp.zeros((ts, CONV_CH), F32) + bdw_ref[...]
    for j in range(CONV_WIDTH):
        acc = acc + wdw_ref[j:j + 1, :] * buf[first + j:first + j + ts, :]
    hn = _layer_norm(acc, gln_ref[...], bln_ref[...])
    hn = hn * jax.nn.sigmoid(hn)
    o = jnp.dot(hn.astype(BF16), wpw_ref[...], preferred_element_type=F32) + bpw_ref[...]
    o_ref[0] = o.astype(BF16)


def _conformer_conv(g, w_dw, b_dw, g_ln, b_ln, w_pw, b_pw):
    b, s, ch = g.shape
    ts = TS_CONV
    per = ts // CONV_HALO
    full = lambda shape: pl.BlockSpec(shape, lambda bi, i: (0,) * len(shape))
    return pl.pallas_call(
        functools.partial(_conv_body, ts=ts),
        grid=(b, s // ts),
        in_specs=[pl.BlockSpec((1, ts, ch), lambda bi, i: (bi, i, 0)),
                  pl.BlockSpec((1, CONV_HALO, ch), lambda bi, i: (bi, jnp.maximum(i * per - 1, 0), 0)),
                  full((CONV_WIDTH, ch)), full((1, ch)), full((1, ch)), full((1, ch)),
                  full((ch, ch)), full((1, ch))],
        out_specs=pl.BlockSpec((1, ts, ch), lambda bi, i: (bi, i, 0)),
        out_shape=jax.ShapeDtypeStruct((b, s, ch), BF16),
        scratch_shapes=[pltpu.VMEM((CONV_HALO + ts, ch), F32)],
        compiler_params=_cparams(("arbitrary", "arbitrary")),
        name="conformer_conv",
    )(g, g, w_dw, b_dw.reshape(1, ch), g_ln.reshape(1, ch), b_ln.reshape(1, ch),
      w_pw.astype(BF16), b_pw.reshape(1, ch))


def _outproj_body(om_ref, oc_ref, os_ref, x_ref, gt_ref, wa_ref, wb_ref, wc_ref, g_ref, b_ref, o_ref, *, alpha):
    mix = (jnp.dot(om_ref[0], wa_ref[...], preferred_element_type=F32)
           + jnp.dot(oc_ref[0], wb_ref[...], preferred_element_type=F32)
           + jnp.dot(os_ref[0], wc_ref[...], preferred_element_type=F32))
    y = alpha * x_ref[0] + (1.0 + gt_ref[0]) * mix
    o_ref[0] = _layer_norm(y, g_ref[...], b_ref[...])


def _prep_outproj_weights(w_o):
    d = w_o.shape[1]
    mla_out = MLA_HEADS * MLA_V
    swa_out = SWA_HEADS * SWA_HEAD_DIM
    wa = w_o[:mla_out].reshape(MLA_HEADS, MLA_V, d)
    wa = jnp.pad(wa, ((0, 0), (0, LANE - MLA_V), (0, 0))).reshape(MLA_HEADS * LANE, d)
    wb = w_o[mla_out:mla_out + CONV_CH]
    wc = w_o[mla_out + CONV_CH:].reshape(SWA_HEADS, SWA_HEAD_DIM, d)
    wc = jnp.pad(wc, ((0, 0), (0, LANE - SWA_HEAD_DIM), (0, 0))).reshape(SWA_HEADS * LANE, d)
    assert wc.shape[0] == SWA_HEADS * LANE and swa_out == SWA_HEADS * SWA_HEAD_DIM
    return wa.astype(BF16), wb.astype(BF16), wc.astype(BF16)


def _out_projection(o_mla, o_conv, o_swa, x, gt1, wa, wb, wc, g_ln, b_ln, alpha):
    b, s, d = x.shape
    tq = TQ_PROJ
    full = lambda shape: pl.BlockSpec(shape, lambda bi, i: (0,) * len(shape))
    tok = lambda w: pl.BlockSpec((1, tq, w), lambda bi, i: (bi, i, 0))
    return pl.pallas_call(
        functools.partial(_outproj_body, alpha=alpha),
        grid=(b, s // tq),
        in_specs=[tok(o_mla.shape[2]), tok(o_conv.shape[2]), tok(o_swa.shape[2]), tok(d),
                  pl.BlockSpec((1, 1, d), lambda bi, i: (bi, 0, 0)),
                  full(wa.shape), full(wb.shape), full(wc.shape), full((1, d)), full((1, d))],
        out_specs=tok(d),
        out_shape=jax.ShapeDtypeStruct((b, s, d), F32),
        compiler_params=_cparams(("arbitrary", "arbitrary")),
        name="out_projection",
    )(o_mla, o_conv, o_swa, x, gt1.reshape(b, 1, d), wa, wb, wc, g_ln.reshape(1, d), b_ln.reshape(1, d))


def _topk_rows(s, k, val_sc, idx_sc, payload=None):
    nrows = s.shape[0]
    rid = lax.broadcasted_iota(I32, s.shape, 0).astype(F32)
    for r in range(k):
        m = jnp.max(s, axis=0, keepdims=True)
        sel = jnp.min(jnp.where(s == m, rid, float(nrows)), axis=0, keepdims=True)
        hit = rid == sel
        val_sc[r:r + 1, :] = m
        if payload is None:
            idx_sc[r:r + 1, :] = sel
        else:
            idx_sc[r:r + 1, :] = jnp.max(jnp.where(hit, payload, -1.0), axis=0, keepdims=True)
        s = jnp.where(hit, -jnp.inf, s)


def _candidate_rows(k):
    return [(a, k // (a + 1)) for a in range(k)]


def _route_body(x_ref, sc_ref, sh_ref, wpq_ref, keys_ref, idx_ref, gate_ref,
                s1_sc, i1_sc, s2_sc, i2_sc, cand_sc, cidx_sc, top_sc, e_sc, idxT_sc, gateT_sc):
    k = PEER_TOPK
    ncand = sum(n for _, n in _candidate_rows(k))
    pad = cand_sc.shape[0] - ncand
    cand_sc[ncand:, :] = jnp.full((pad, cand_sc.shape[1]), -jnp.inf, F32)
    cidx_sc[ncand:, :] = jnp.full((pad, cand_sc.shape[1]), -1.0, F32)
    h = x_ref[0] * (1.0 + sc_ref[0]) + sh_ref[0]
    qp = jnp.dot(h.astype(BF16), wpq_ref[...], preferred_element_type=F32)
    for hd in range(PEER_HEADS):
        for half, (v_sc, i_sc) in enumerate(((s1_sc, i1_sc), (s2_sc, i2_sc))):
            col = (hd * 2 + half) * LANE
            qh = qp[:, col:col + LANE].astype(BF16)
            sc = lax.dot_general(keys_ref[hd * 2 + half], qh, _NT, preferred_element_type=F32)
            _topk_rows(sc, k, v_sc, i_sc)
        r0 = 0
        for a, n in _candidate_rows(k):
            cand_sc[r0:r0 + n, :] = s1_sc[a:a + 1, :] + s2_sc[0:n, :]
            cidx_sc[r0:r0 + n, :] = i1_sc[a:a + 1, :] * float(N_SUBKEYS) + i2_sc[0:n, :]
            r0 += n
        _topk_rows(cand_sc[...], k, top_sc, e_sc, payload=cidx_sc[...])
        top = top_sc[...]
        ex = jnp.exp(top - top[0:1, :])
        gateT_sc[hd * k:(hd + 1) * k, :] = ex / jnp.sum(ex, axis=0, keepdims=True)
        idxT_sc[hd * k:(hd + 1) * k, :] = (e_sc[...] * float(ROWS_PER_EXPERT)).astype(I32)
    idx_ref[...] = idxT_sc[...].T
    gate_ref[...] = gateT_sc[...].T


def _peer_route(x1, sc2, sh2, w_pq, keys):
    b, s, d = x1.shape
    tt = TT_ROUTE
    k = PEER_TOPK
    nq = w_pq.shape[1]
    per = s // tt
    ncand = sum(n for _, n in _candidate_rows(k))
    ncand_pad = -(-ncand // SUBLANE) * SUBLANE
    full = lambda shape: pl.BlockSpec(shape, lambda bi, i: (0,) * len(shape))
    vec = pl.BlockSpec((1, 1, d), lambda bi, i: (bi, 0, 0))
    out = pl.BlockSpec((tt, PEER_HK), lambda bi, i: (bi * per + i, 0))
    return pl.pallas_call(
        _route_body,
        grid=(b, per),
        in_specs=[pl.BlockSpec((1, tt, d), lambda bi, i: (bi, i, 0)), vec, vec, full((d, nq)),
                  full(keys.shape)],
        out_specs=[out, out],
        out_shape=[jax.ShapeDtypeStruct((b * s, PEER_HK), I32), jax.ShapeDtypeStruct((b * s, PEER_HK), F32)],
        scratch_shapes=[pltpu.VMEM((k, tt), F32)] * 4
                       + [pltpu.VMEM((ncand_pad, tt), F32)] * 2
                       + [pltpu.VMEM((k, tt), F32)] * 2
                       + [pltpu.VMEM((PEER_HK, tt), I32), pltpu.VMEM((PEER_HK, tt), F32)],
        compiler_params=_cparams(("arbitrary", "arbitrary")),
        name="peer_route",
    )(x1, sc2.reshape(b, 1, d), sh2.reshape(b, 1, d), w_pq, keys)


def _pack_table(tab):
    n, d = tab.shape
    t16 = lax.bitcast_convert_type(tab.astype(BF16), jnp.uint16).astype(jnp.uint32)
    words = (t16[:, :d // 2] << 16) | t16[:, d // 2:]
    return lax.bitcast_convert_type(words, I32).reshape(n * ROWS_PER_EXPERT, LANE)


def _unpack(words):
    hi = lax.bitcast_convert_type(words & jnp.int32(-65536), F32)
    lo = lax.bitcast_convert_type(words << 16, F32)
    return hi, lo


def _load_table(tab_hbm, tab_vmem, sem):
    @pl.when(pl.program_id(0) == 0)
    def _():
        cp = pltpu.make_async_copy(tab_hbm, tab_vmem, sem)
        cp.start()
        cp.wait()


def _gather_rows(idx_ref, tab_vmem, t, gbuf):
    idx_t = idx_ref.at[pl.ds(t * PEER_HK, PEER_HK)]
    for j in range(PEER_HK):
        r = pl.multiple_of(idx_t[j], ROWS_PER_EXPERT)
        gbuf[j * ROWS_PER_EXPERT:(j + 1) * ROWS_PER_EXPERT, :] = tab_vmem[pl.ds(r, ROWS_PER_EXPERT), :]


def _pipelined_tokens(tn, gather, compute, buf_a, buf_b):
    gather(0, buf_a)

    def pair(i, carry):
        t = 2 * i
        gather(t + 1, buf_b)
        compute(t, buf_a)
        gather(jnp.minimum(t + 2, tn - 1), buf_a)
        compute(t + 1, buf_b)
        return carry

    lax.fori_loop(0, tn // 2, pair, 0)


def _peer_u_body(idx_ref, x_ref, sc_ref, sh_ref, gate_ref, tab_hbm, w_ref, tab_vmem, buf_a, buf_b, sem, *, tn):
    _load_table(tab_hbm, tab_vmem, sem)
    scale = 1.0 + sc_ref[0]
    shift = sh_ref[0]

    def compute(t, gbuf):
        h = x_ref[pl.ds(pl.multiple_of(t * SUBLANE, SUBLANE), SUBLANE), :] * scale + shift
        acc = jnp.zeros((PEER_HK, LANE), F32)
        for c in range(ROWS_PER_EXPERT):
            hi, lo = _unpack(gbuf[pl.ds(c, PEER_HK, stride=ROWS_PER_EXPERT), :])
            acc = acc + hi * h[c:c + 1, :] + lo * h[ROWS_PER_EXPERT + c:ROWS_PER_EXPERT + c + 1, :]
        srow = jnp.sum(acc.T, axis=0, keepdims=True)
        act = 0.5 * srow * (1.0 + lax.erf(srow * (2.0 ** -0.5)))
        w_ref[pl.ds(t, 1), :] = gate_ref[pl.ds(t, 1), :] * act

    _pipelined_tokens(tn, functools.partial(_gather_rows, idx_ref, tab_vmem), compute, buf_a, buf_b)


def _peer_v_body(idx_ref, w_ref, tab_hbm, y_ref, tab_vmem, buf_a, buf_b, sem, *, tn):
    _load_table(tab_hbm, tab_vmem, sem)

    def compute(t, gbuf):
        wrow = w_ref[pl.ds(t, 1), :]
        wrep = jnp.broadcast_to(wrow, (LANE, PEER_HK)).T
        his, los = [], []
        for c in range(ROWS_PER_EXPERT):
            hi, lo = _unpack(gbuf[pl.ds(c, PEER_HK, stride=ROWS_PER_EXPERT), :])
            his.append(jnp.sum(wrep * hi, axis=0, keepdims=True))
            los.append(jnp.sum(wrep * lo, axis=0, keepdims=True))
        y_ref[pl.ds(pl.multiple_of(t * SUBLANE, SUBLANE), SUBLANE), :] = jnp.concatenate(his + los, axis=0)

    _pipelined_tokens(tn, functools.partial(_gather_rows, idx_ref, tab_vmem), compute, buf_a, buf_b)


def _peer_experts(idx, gate, x1, sc2, sh2, u_pack, v_pack):
    b, s, d = x1.shape
    t = b * s
    tn = TN_PEER
    per = s // tn
    rows = d // LANE
    assert rows == SUBLANE and d == 2 * ROWS_PER_EXPERT * LANE
    idx_flat = idx.reshape(t * PEER_HK)
    idx_spec = pl.BlockSpec((tn * PEER_HK,), lambda i: (i,), memory_space=pltpu.SMEM)
    tokrows = pl.BlockSpec((tn * rows, LANE), lambda i: (i, 0))
    tokhk = pl.BlockSpec((tn, PEER_HK), lambda i: (i, 0))
    vec = pl.BlockSpec((1, rows, LANE), lambda i: (i // per, 0, 0))
    table = pl.BlockSpec(memory_space=pl.ANY)
    gbuf = pltpu.VMEM((PEER_HK * ROWS_PER_EXPERT, LANE), I32)
    scratch = [pltpu.VMEM(u_pack.shape, I32), gbuf, gbuf, pltpu.SemaphoreType.DMA(())]
    w = pl.pallas_call(
        functools.partial(_peer_u_body, tn=tn),
        grid=(t // tn,),
        in_specs=[idx_spec, tokrows, vec, vec, tokhk, table],
        out_specs=tokhk,
        out_shape=jax.ShapeDtypeStruct((t, PEER_HK), F32),
        scratch_shapes=scratch,
        compiler_params=_cparams(("arbitrary",), VMEM_LIMIT_TABLE),
        name="peer_expert_in",
    )(idx_flat, x1.reshape(t * rows, LANE), sc2.reshape(b, rows, LANE), sh2.reshape(b, rows, LANE), gate, u_pack)
    y = pl.pallas_call(
        functools.partial(_peer_v_body, tn=tn),
        grid=(t // tn,),
        in_specs=[idx_spec, tokhk, table],
        out_specs=tokrows,
        out_shape=jax.ShapeDtypeStruct((t * rows, LANE), F32),
        scratch_shapes=scratch,
        compiler_params=_cparams(("arbitrary",), VMEM_LIMIT_TABLE),
        name="peer_expert_out",
    )(idx_flat, w, v_pack)
    return y.reshape(b, s, d)


def _resnorm_body(x_ref, y_ref, gt_ref, g_ref, b_ref, o_ref, *, alpha):
    v = alpha * x_ref[0] + (1.0 + gt_ref[0]) * y_ref[0]
    o_ref[0] = _layer_norm(v, g_ref[...], b_ref[...])


def _residual_norm(x, y, gt, g_ln, b_ln, alpha):
    b, s, d = x.shape
    tq = TQ_PROJ
    tok = pl.BlockSpec((1, tq, d), lambda bi, i: (bi, i, 0))
    full = pl.BlockSpec((1, d), lambda bi, i: (0, 0))
    return pl.pallas_call(
        functools.partial(_resnorm_body, alpha=alpha),
        grid=(b, s // tq),
        in_specs=[tok, tok, pl.BlockSpec((1, 1, d), lambda bi, i: (bi, 0, 0)), full, full],
        out_specs=tok,
        out_shape=jax.ShapeDtypeStruct((b, s, d), F32),
        compiler_params=_cparams(("arbitrary", "arbitrary")),
        name="residual_norm",
    )(x, y, gt.reshape(b, 1, d), g_ln.reshape(1, d), b_ln.reshape(1, d))


def kernel(x, c, w_ada, b_ada, w_in, g_q, w_uq, g_kv, w_ukv, w_dw, b_dw, g_conv, b_conv, w_pw, b_pw, sinks,
           w_o, g_ln1, b_ln1, w_pq, sub_keys, u_tab, v_tab, g_ln2, b_ln2):
    depth = w_ada.shape[0]
    b, s, d = x.shape
    alpha = float((2 * depth) ** 0.25)
    cos_t, sin_t = _rope_tables(s)
    mod = _ada_mod(c, w_ada, b_ada)
    for l in range(depth):
        sh1, sc1, gt1, sh2, sc2, gt2 = [mod[l, :, i * d:(i + 1) * d] for i in range(6)]
        w_in_p, w_q2, w_kv2 = _prep_inproj_weights(w_in[l], w_uq[l], w_ukv[l])
        q, k, v, g, qs, ks, vs = _in_projection(x, sc1, sh1, w_in_p, g_q[l], w_q2, g_kv[l], w_kv2, cos_t, sin_t)
        o_mla = _mla_attention(q, k, v)
        o_swa = _swa_attention(qs, ks, vs, sinks[l])
        o_conv = _conformer_conv(g, w_dw[l], b_dw[l], g_conv[l], b_conv[l], w_pw[l], b_pw[l])
        wa, wb, wc = _prep_outproj_weights(w_o[l])
        x = _out_projection(o_mla, o_conv, o_swa, x, gt1, wa, wb, wc, g_ln1[l], b_ln1[l], alpha)

        keys = sub_keys[l].reshape(PEER_HEADS * 2, N_SUBKEYS, -1).astype(BF16)
        idx, gate = _peer_route(x, sc2, sh2, w_pq[l].astype(BF16), keys)
        y = _peer_experts(idx, gate, x, sc2, sh2, _pack_table(u_tab[l]), _pack_table(v_tab[l]))
        x = _residual_norm(x, y, gt2, g_ln2[l], b_ln2[l], alpha)
    return x
```

```python
import functools

import jax
import jax.numpy as jnp
from jax import lax
from jax.experimental import pallas as pl
from jax.experimental.pallas import tpu as pltpu

F32 = jnp.float32
BF16 = jnp.bfloat16
I32 = jnp.int32

LANE = 128
SUBLANE = 8
VMEM_BYTES_V7X = 64 * 1024 * 1024
VMEM_LIMIT = 48 * 1024 * 1024
VMEM_LIMIT_TABLE = 56 * 1024 * 1024

MLA_HEADS = 8
MLA_NOPE = 64
MLA_ROPE = 32
MLA_V = 64
MLA_Q_RANK = 256
MLA_KV_RANK = 128
ROPE_BASE = 10000.0
CONV_CH = 256
CONV_WIDTH = 31
SWA_HEADS = 4
SWA_KV_HEADS = 2
SWA_HEAD_DIM = 64
BLOCK = 128
IN_SPLITS = (MLA_Q_RANK, MLA_KV_RANK, MLA_ROPE, 2 * CONV_CH,
             SWA_HEADS * SWA_HEAD_DIM, SWA_KV_HEADS * SWA_HEAD_DIM, SWA_KV_HEADS * SWA_HEAD_DIM)
N_SUBKEYS = 128
PEER_HEADS = 8
PEER_TOPK = 16
PEER_HK = PEER_HEADS * PEER_TOPK
EPS = 1e-6

TQ_PROJ = 512
TQ_ATT = 512
MLA_ROW_SPLIT = 2
TS_SWA = 512
TS_CONV = 512
CONV_HALO = 32
TT_ROUTE = 256
TN_PEER = 256
ROWS_PER_EXPERT = 4
GATHER_GROUP = 8
PEER_PIPE = 2

_NT = (((1,), (1,)), ((), ()))


def _cparams(sem, limit=VMEM_LIMIT):
    return pltpu.CompilerParams(dimension_semantics=sem, vmem_limit_bytes=limit)


def _layer_norm(v, g, b):
    mu = jnp.mean(v, axis=-1, keepdims=True)
    d = v - mu
    var = jnp.mean(d * d, axis=-1, keepdims=True)
    return d * lax.rsqrt(var + EPS) * g + b


def _rms_norm(v, g):
    return v * lax.rsqrt(jnp.mean(v * v, axis=-1, keepdims=True) + EPS) * g


def _ada_body(c_ref, w_ref, b_ref, o_ref):
    c = c_ref[...]
    cs = c * jax.nn.sigmoid(c)
    o_ref[0] = jnp.dot(cs, w_ref[0], preferred_element_type=F32,
                       precision=lax.Precision.HIGHEST) + b_ref[0]


def _ada_mod(c, w_ada, b_ada):
    depth, d, d6 = w_ada.shape
    bp = -(-c.shape[0] // SUBLANE) * SUBLANE
    cp = jnp.pad(c, ((0, bp - c.shape[0]), (0, 0)))
    tn = d
    out = pl.pallas_call(
        _ada_body,
        grid=(depth, d6 // tn),
        in_specs=[pl.BlockSpec((bp, d), lambda l, j: (0, 0)),
                  pl.BlockSpec((1, d, tn), lambda l, j: (l, 0, j)),
                  pl.BlockSpec((1, 1, tn), lambda l, j: (l, 0, j))],
        out_specs=pl.BlockSpec((1, bp, tn), lambda l, j: (l, 0, j)),
        out_shape=jax.ShapeDtypeStruct((depth, bp, d6), F32),
        compiler_params=_cparams(("arbitrary", "arbitrary")),
        name="ada_mod",
    )(cp, w_ada, b_ada.reshape(depth, 1, d6))
    return out[:, :c.shape[0]]


def _inproj_body(x_ref, sc_ref, sh_ref, win_ref, gq_ref, wq_ref, gkv_ref, wkv_ref, cos_ref, sin_ref,
                 q_ref, k_ref, v_ref, g_ref, qs_ref, ks_ref, vs_ref, *, mla_scale, swa_scale):
    hp = MLA_HEADS * LANE
    h = x_ref[0] * (1.0 + sc_ref[0]) + sh_ref[0]
    proj = jnp.dot(h.astype(BF16), win_ref[...], preferred_element_type=F32)
    o = 0
    c_q = proj[:, o:o + MLA_Q_RANK]; o += MLA_Q_RANK
    c_kv = proj[:, o:o + MLA_KV_RANK]; o += MLA_KV_RANK
    kr = proj[:, o:o + LANE]; o += LANE
    kr_sw = proj[:, o:o + LANE]; o += LANE
    a = proj[:, o:o + CONV_CH]; o += CONV_CH
    gate = proj[:, o:o + CONV_CH]; o += CONV_CH
    qs = proj[:, o:o + SWA_HEADS * LANE]; o += SWA_HEADS * LANE
    ks = proj[:, o:o + SWA_KV_HEADS * LANE]; o += SWA_KV_HEADS * LANE
    vs = proj[:, o:o + SWA_KV_HEADS * LANE]

    cos = cos_ref[...]
    sin = sin_ref[...]
    nq = _rms_norm(c_q, gq_ref[...]).astype(BF16)
    q2 = jnp.dot(nq, wq_ref[...], preferred_element_type=F32)
    nkv = _rms_norm(c_kv, gkv_ref[...]).astype(BF16)
    kv2 = jnp.dot(nkv, wkv_ref[...], preferred_element_type=F32)
    k_rope = kr * cos + kr_sw * sin
    for hd in range(MLA_HEADS):
        sl = slice(hd * LANE, (hd + 1) * LANE)
        sl2 = slice(hp + hd * LANE, hp + (hd + 1) * LANE)
        q_ref[0, :, sl] = ((q2[:, sl] * cos + q2[:, sl2] * sin) * mla_scale).astype(BF16)
        k_ref[0, :, sl] = (kv2[:, sl] + k_rope).astype(BF16)
    lane = lax.broadcasted_iota(I32, (1, hp), 1)
    ones_col = jnp.where((lane & (LANE - 1)) == MLA_V, 1.0, 0.0)
    v_ref[0] = (kv2[:, hp:] + ones_col).astype(BF16)
    g_ref[0] = a * jax.nn.sigmoid(gate)
    qs_ref[0] = (qs * swa_scale).astype(BF16)
    ks_ref[0] = ks.astype(BF16)
    vs_ref[0] = vs.astype(BF16)


def _pad_heads(w, heads, width):
    r = w.shape[0]
    w = w.reshape(r, heads, width)
    return jnp.pad(w, ((0, 0), (0, 0), (0, LANE - width))).reshape(r, heads * LANE)


def _rope_place(r1, r2):
    z64 = jnp.zeros(r1.shape[:-1] + (MLA_NOPE,), r1.dtype)
    z32 = jnp.zeros(r1.shape[:-1] + (LANE - MLA_NOPE - MLA_ROPE,), r1.dtype)
    return jnp.concatenate([z64, r1, r2, z32], axis=-1)


def _prep_inproj_weights(w_in, w_uq, w_ukv):
    d = w_in.shape[0]
    half = MLA_ROPE // 2
    offs = [0]
    for s in IN_SPLITS:
        offs.append(offs[-1] + s)
    c_q, c_kv, k_r, conv, q_s, k_s, v_s = [w_in[:, offs[i]:offs[i + 1]] for i in range(7)]
    kr1, kr2 = k_r[:, :half], k_r[:, half:]
    w_in_p = jnp.concatenate([
        c_q, c_kv, _rope_place(kr1, kr2), _rope_place(-kr2, kr1), conv,
        _pad_heads(q_s, SWA_HEADS, SWA_HEAD_DIM), _pad_heads(k_s, SWA_KV_HEADS, SWA_HEAD_DIM),
        _pad_heads(v_s, SWA_KV_HEADS, SWA_HEAD_DIM)], axis=1).astype(BF16)

    wq = w_uq.reshape(MLA_Q_RANK, MLA_HEADS, MLA_NOPE + MLA_ROPE)
    nope, r1, r2 = wq[..., :MLA_NOPE], wq[..., MLA_NOPE:MLA_NOPE + half], wq[..., MLA_NOPE + half:]
    zpad = jnp.zeros(nope.shape[:-1] + (LANE - MLA_NOPE - MLA_ROPE,), nope.dtype)
    placed = jnp.concatenate([nope, r1, r2, zpad], axis=-1).reshape(MLA_Q_RANK, MLA_HEADS * LANE)
    partner = _rope_place(-r2, r1).reshape(MLA_Q_RANK, MLA_HEADS * LANE)
    w_q2 = jnp.concatenate([placed, partner], axis=1).astype(BF16)

    wkv = w_ukv.reshape(MLA_KV_RANK, MLA_HEADS, MLA_NOPE + MLA_V)
    kn = jnp.pad(wkv[..., :MLA_NOPE], ((0, 0), (0, 0), (0, LANE - MLA_NOPE)))
    vv = jnp.pad(wkv[..., MLA_NOPE:], ((0, 0), (0, 0), (0, LANE - MLA_V)))
    w_kv2 = jnp.concatenate([kn.reshape(MLA_KV_RANK, -1), vv.reshape(MLA_KV_RANK, -1)], axis=1).astype(BF16)
    return w_in_p, w_q2, w_kv2


def _rope_tables(seq):
    half = MLA_ROPE // 2
    inv = 1.0 / (ROPE_BASE ** (jnp.arange(0, MLA_ROPE, 2, dtype=F32) / MLA_ROPE))
    ang = jnp.arange(seq, dtype=F32)[:, None] * inv[None, :]
    cos, sin = jnp.cos(ang), jnp.sin(ang)
    ones = jnp.ones((seq, MLA_NOPE), F32)
    zpad = jnp.zeros((seq, LANE - MLA_NOPE - MLA_ROPE), F32)
    cos_t = jnp.concatenate([ones, cos, cos, zpad], axis=1)
    sin_t = jnp.concatenate([jnp.zeros((seq, MLA_NOPE), F32), sin, sin, zpad], axis=1)
    assert half * 2 == MLA_ROPE
    return cos_t, sin_t


def _in_projection(x, sc1, sh1, w_in_p, g_q, w_q2, g_kv, w_kv2, cos_t, sin_t):
    b, s, d = x.shape
    tq = TQ_PROJ
    ncol = w_in_p.shape[1]
    hp = MLA_HEADS * LANE
    full = lambda shape: pl.BlockSpec(shape, lambda bi, i: (0,) * len(shape))
    tok = lambda w: pl.BlockSpec((1, tq, w), lambda bi, i: (bi, i, 0))
    vec = pl.BlockSpec((1, 1, d), lambda bi, i: (bi, 0, 0))
    body = functools.partial(_inproj_body, mla_scale=float((MLA_NOPE + MLA_ROPE) ** -0.5),
                             swa_scale=float(SWA_HEAD_DIM ** -0.5))
    return pl.pallas_call(
        body,
        grid=(b, s // tq),
        in_specs=[tok(d), vec, vec, full((d, ncol)), full((1, MLA_Q_RANK)), full((MLA_Q_RANK, 2 * hp)),
                  full((1, MLA_KV_RANK)), full((MLA_KV_RANK, 2 * hp)),
                  pl.BlockSpec((tq, LANE), lambda bi, i: (i, 0)),
                  pl.BlockSpec((tq, LANE), lambda bi, i: (i, 0))],
        out_specs=[tok(hp), tok(hp), tok(hp), tok(CONV_CH), tok(SWA_HEADS * LANE),
                   tok(SWA_KV_HEADS * LANE), tok(SWA_KV_HEADS * LANE)],
        out_shape=[jax.ShapeDtypeStruct((b, s, hp), BF16), jax.ShapeDtypeStruct((b, s, hp), BF16),
                   jax.ShapeDtypeStruct((b, s, hp), BF16), jax.ShapeDtypeStruct((b, s, CONV_CH), F32),
                   jax.ShapeDtypeStruct((b, s, SWA_HEADS * LANE), BF16),
                   jax.ShapeDtypeStruct((b, s, SWA_KV_HEADS * LANE), BF16),
                   jax.ShapeDtypeStruct((b, s, SWA_KV_HEADS * LANE), BF16)],
        compiler_params=_cparams(("arbitrary", "arbitrary")),
        name="in_projection",
    )(x, sc1.reshape(b, 1, d), sh1.reshape(b, 1, d), w_in_p, g_q.reshape(1, -1), w_q2,
      g_kv.reshape(1, -1), w_kv2, cos_t, sin_t)


def _fold_lane_tiles(v, op):
    out = v[:, 0:LANE]
    for c in range(1, v.shape[1] // LANE):
        out = op(out, v[:, c * LANE:(c + 1) * LANE])
    return out


def _mla_body(q_ref, k_ref, v_ref, o_ref, m_sc, acc_sc, s_a, s_b, *, tq):
    qi = pl.program_id(2)
    m_sc[...] = jnp.full(m_sc.shape, -jnp.inf, F32)
    acc_sc[...] = jnp.zeros(acc_sc.shape, F32)
    tr = tq // MLA_ROW_SPLIT

    def scores(j, s_ref):
        k = k_ref[0, pl.ds(pl.multiple_of(j * tq, tq), tq), :]
        s_ref[...] = lax.dot_general(q_ref[0], k, _NT, preferred_element_type=F32)

    def accumulate(s_ref, j, diagonal):
        v = v_ref[0, pl.ds(pl.multiple_of(j * tq, tq), tq), :]
        for r in range(MLA_ROW_SPLIT):
            rows = slice(r * tr, (r + 1) * tr)
            nk = (r + 1) * tr if diagonal else tq
            s = s_ref[rows, 0:nk]
            if diagonal:
                row = lax.broadcasted_iota(I32, s.shape, 0) + r * tr
                col = lax.broadcasted_iota(I32, s.shape, 1)
                s = jnp.where(col <= row, s, -jnp.inf)
            m_prev = m_sc[rows, :]
            m_new = jnp.maximum(m_prev, jnp.max(_fold_lane_tiles(s, jnp.maximum), axis=1, keepdims=True))
            alpha = jnp.exp(m_prev - m_new)
            p = jnp.exp(s - jnp.concatenate([m_new] * (nk // LANE), axis=1))
            acc_sc[rows, :] = alpha * acc_sc[rows, :] + jnp.dot(p.astype(BF16), v[0:nk], preferred_element_type=F32)
            m_sc[rows, :] = m_new

    scores(0, s_a)

    def pair(i, carry):
        j = 2 * i
        scores(j + 1, s_b)
        accumulate(s_a, j, False)
        scores(j + 2, s_a)
        accumulate(s_b, j + 1, False)
        return carry

    lax.fori_loop(0, lax.shift_right_logical(qi, 1), pair, 0)

    @pl.when((qi & 1) == 0)
    def _():
        accumulate(s_a, qi, True)

    @pl.when((qi & 1) == 1)
    def _():
        scores(qi, s_b)
        accumulate(s_a, qi - 1, False)
        accumulate(s_b, qi, True)

    acc = acc_sc[...]
    o_ref[0] = (acc / acc[:, MLA_V:MLA_V + 1]).astype(BF16)


def _mla_attention(q, k, v):
    b, s, hp = q.shape
    tq = TQ_ATT
    return pl.pallas_call(
        functools.partial(_mla_body, tq=tq),
        grid=(b, MLA_HEADS, s // tq),
        in_specs=[pl.BlockSpec((1, tq, LANE), lambda bi, h, i: (bi, i, h)),
                  pl.BlockSpec((1, s, LANE), lambda bi, h, i: (bi, 0, h)),
                  pl.BlockSpec((1, s, LANE), lambda bi, h, i: (bi, 0, h))],
        out_specs=pl.BlockSpec((1, tq, LANE), lambda bi, h, i: (bi, i, h)),
        out_shape=jax.ShapeDtypeStruct((b, s, hp), BF16),
        scratch_shapes=[pltpu.VMEM((tq, LANE), F32), pltpu.VMEM((tq, LANE), F32),
                        pltpu.VMEM((tq, tq), F32), pltpu.VMEM((tq, tq), F32)],
        compiler_params=_cparams(("arbitrary", "arbitrary", "arbitrary")),
        name="mla_attention",
    )(q, k, v)


def _swa_body(sink_ref, q_ref, k_ref, kh_ref, v_ref, vh_ref, o_ref, *, nsub):
    i = pl.program_id(1)
    row = lax.broadcasted_iota(I32, (BLOCK, BLOCK), 0)
    col = lax.broadcasted_iota(I32, (BLOCK, BLOCK), 1)
    cur_ok = col <= row
    prev_band = col > row
    group = SWA_HEADS // SWA_KV_HEADS
    for j in range(nsub):
        rows = slice(j * BLOCK, (j + 1) * BLOCK)
        prev_ok = jnp.logical_and(prev_band, i > 0) if j == 0 else prev_band
        for g in range(SWA_KV_HEADS):
            lanes = slice(g * LANE, (g + 1) * LANE)
            if j == 0:
                kp, vp = kh_ref[0, :, lanes], vh_ref[0, :, lanes]
            else:
                prows = slice((j - 1) * BLOCK, j * BLOCK)
                kp, vp = k_ref[0, prows, lanes], v_ref[0, prows, lanes]
            kc, vc = k_ref[0, rows, lanes], v_ref[0, rows, lanes]
            for a in range(group):
                hq = g * group + a
                ql = slice(hq * LANE, (hq + 1) * LANE)
                qq = q_ref[0, rows, ql]
                sp = lax.dot_general(qq, kp, _NT, preferred_element_type=F32)
                sc = lax.dot_general(qq, kc, _NT, preferred_element_type=F32)
                sp = jnp.where(prev_ok, sp, -jnp.inf)
                sc = jnp.where(cur_ok, sc, -jnp.inf)
                sink = sink_ref[hq]
                m = jnp.maximum(jnp.maximum(jnp.max(sp, axis=1, keepdims=True),
                                            jnp.max(sc, axis=1, keepdims=True)), sink)
                pp = jnp.exp(sp - m)
                pc = jnp.exp(sc - m)
                den = (jnp.sum(pp, axis=1, keepdims=True) + jnp.sum(pc, axis=1, keepdims=True)
                       + jnp.exp(sink - m))
                o = (jnp.dot(pp.astype(BF16), vp, preferred_element_type=F32)
                     + jnp.dot(pc.astype(BF16), vc, preferred_element_type=F32)) / den
                o_ref[0, rows, ql] = o.astype(BF16)


def _swa_attention(qs, ks, vs, sinks):
    b, s, _ = qs.shape
    ts = TS_SWA
    nsub = ts // BLOCK
    halo = lambda w: pl.BlockSpec((1, BLOCK, w), lambda bi, i: (bi, jnp.maximum(i * nsub - 1, 0), 0))
    tok = lambda w: pl.BlockSpec((1, ts, w), lambda bi, i: (bi, i, 0))
    wq, wk = SWA_HEADS * LANE, SWA_KV_HEADS * LANE
    return pl.pallas_call(
        functools.partial(_swa_body, nsub=nsub),
        grid=(b, s // ts),
        in_specs=[pl.BlockSpec(memory_space=pltpu.SMEM), tok(wq), tok(wk), halo(wk), tok(wk), halo(wk)],
        out_specs=tok(wq),
        out_shape=jax.ShapeDtypeStruct((b, s, wq), BF16),
        compiler_params=_cparams(("arbitrary", "arbitrary")),
        name="swa_attention",
    )(sinks, qs, ks, ks, vs, vs)


def _conv_body(g_ref, gh_ref, wdw_ref, bdw_ref, gln_ref, bln_ref, wpw_ref, bpw_ref, o_ref, buf, *, ts):
    i = pl.program_id(1)
    halo = gh_ref[0]
    buf[0:CONV_HALO, :] = jnp.where(i > 0, halo, jnp.zeros_like(halo))
    buf[CONV_HALO:CONV_HALO + ts, :] = g_ref[0]
    first = CONV_HALO - (CONV_WIDTH - 1)
    acc = jnp.zeros((ts, CONV_CH), F32) + bdw_ref[...]
    for j in range(CONV_WIDTH):
        acc = acc + wdw_ref[j:j + 1, :] * buf[first + j:first + j + ts, :]
    hn = _layer_norm(acc, gln_ref[...], bln_ref[...])
    hn = hn * jax.nn.sigmoid(hn)
    o = jnp.dot(hn.astype(BF16), wpw_ref[...], preferred_element_type=F32) + bpw_ref[...]
    o_ref[0] = o.astype(BF16)


def _conformer_conv(g, w_dw, b_dw, g_ln, b_ln, w_pw, b_pw):
    b, s, ch = g.shape
    ts = TS_CONV
    per = ts // CONV_HALO
    full = lambda shape: pl.BlockSpec(shape, lambda bi, i: (0,) * len(shape))
    return pl.pallas_call(
        functools.partial(_conv_body, ts=ts),
        grid=(b, s // ts),
        in_specs=[pl.BlockSpec((1, ts, ch), lambda bi, i: (bi, i, 0)),
                  pl.BlockSpec((1, CONV_HALO, ch), lambda bi, i: (bi, jnp.maximum(i * per - 1, 0), 0)),
                  full((CONV_WIDTH, ch)), full((1, ch)), full((1, ch)), full((1, ch)),
                  full((ch, ch)), full((1, ch))],
        out_specs=pl.BlockSpec((1, ts, ch), lambda bi, i: (bi, i, 0)),
        out_shape=jax.ShapeDtypeStruct((b, s, ch), BF16),
        scratch_shapes=[pltpu.VMEM((CONV_HALO + ts, ch), F32)],
        compiler_params=_cparams(("arbitrary", "arbitrary")),
        name="conformer_conv",
    )(g, g, w_dw, b_dw.reshape(1, ch), g_ln.reshape(1, ch), b_ln.reshape(1, ch),
      w_pw.astype(BF16), b_pw.reshape(1, ch))


def _outproj_body(om_ref, oc_ref, os_ref, x_ref, gt_ref, wa_ref, wb_ref, wc_ref, g_ref, b_ref, o_ref, *, alpha):
    mix = (jnp.dot(om_ref[0], wa_ref[...], preferred_element_type=F32)
           + jnp.dot(oc_ref[0], wb_ref[...], preferred_element_type=F32)
           + jnp.dot(os_ref[0], wc_ref[...], preferred_element_type=F32))
    y = alpha * x_ref[0] + (1.0 + gt_ref[0]) * mix
    o_ref[0] = _layer_norm(y, g_ref[...], b_ref[...])


def _prep_outproj_weights(w_o):
    d = w_o.shape[1]
    mla_out = MLA_HEADS * MLA_V
    swa_out = SWA_HEADS * SWA_HEAD_DIM
    wa = w_o[:mla_out].reshape(MLA_HEADS, MLA_V, d)
    wa = jnp.pad(wa, ((0, 0), (0, LANE - MLA_V), (0, 0))).reshape(MLA_HEADS * LANE, d)
    wb = w_o[mla_out:mla_out + CONV_CH]
    wc = w_o[mla_out + CONV_CH:].reshape(SWA_HEADS, SWA_HEAD_DIM, d)
    wc = jnp.pad(wc, ((0, 0), (0, LANE - SWA_HEAD_DIM), (0, 0))).reshape(SWA_HEADS * LANE, d)
    assert wc.shape[0] == SWA_HEADS * LANE and swa_out == SWA_HEADS * SWA_HEAD_DIM
    return wa.astype(BF16), wb.astype(BF16), wc.astype(BF16)


def _out_projection(o_mla, o_conv, o_swa, x, gt1, wa, wb, wc, g_ln, b_ln, alpha):
    b, s, d = x.shape
    tq = TQ_PROJ
    full = lambda shape: pl.BlockSpec(shape, lambda bi, i: (0,) * len(shape))
    tok = lambda w: pl.BlockSpec((1, tq, w), lambda bi, i: (bi, i, 0))
    return pl.pallas_call(
        functools.partial(_outproj_body, alpha=alpha),
        grid=(b, s // tq),
        in_specs=[tok(o_mla.shape[2]), tok(o_conv.shape[2]), tok(o_swa.shape[2]), tok(d),
                  pl.BlockSpec((1, 1, d), lambda bi, i: (bi, 0, 0)),
                  full(wa.shape), full(wb.shape), full(wc.shape), full((1, d)), full((1, d))],
        out_specs=tok(d),
        out_shape=jax.ShapeDtypeStruct((b, s, d), F32),
        compiler_params=_cparams(("arbitrary", "arbitrary")),
        name="out_projection",
    )(o_mla, o_conv, o_swa, x, gt1.reshape(b, 1, d), wa, wb, wc, g_ln.reshape(1, d), b_ln.reshape(1, d))


def _topk_rows(s, k, val_sc, idx_sc, payload=None):
    nrows = s.shape[0]
    rid = lax.broadcasted_iota(I32, s.shape, 0).astype(F32)
    for r in range(k):
        m = jnp.max(s, axis=0, keepdims=True)
        sel = jnp.min(jnp.where(s == m, rid, float(nrows)), axis=0, keepdims=True)
        hit = rid == sel
        val_sc[r:r + 1, :] = m
        if payload is None:
            idx_sc[r:r + 1, :] = sel
        else:
            idx_sc[r:r + 1, :] = jnp.max(jnp.where(hit, payload, -1.0), axis=0, keepdims=True)
        s = jnp.where(hit, -jnp.inf, s)


def _candidate_rows(k):
    return [(a, k // (a + 1)) for a in range(k)]


def _route_body(x_ref, sc_ref, sh_ref, wpq_ref, keys_ref, idx_ref, gate_ref,
                s1_sc, i1_sc, s2_sc, i2_sc, cand_sc, cidx_sc, top_sc, e_sc, idxT_sc, gateT_sc):
    k = PEER_TOPK
    ncand = sum(n for _, n in _candidate_rows(k))
    pad = cand_sc.shape[0] - ncand
    cand_sc[ncand:, :] = jnp.full((pad, cand_sc.shape[1]), -jnp.inf, F32)
    cidx_sc[ncand:, :] = jnp.full((pad, cand_sc.shape[1]), -1.0, F32)
    h = x_ref[0] * (1.0 + sc_ref[0]) + sh_ref[0]
    qp = jnp.dot(h.astype(BF16), wpq_ref[...], preferred_element_type=F32)
    for hd in range(PEER_HEADS):
        for half, (v_sc, i_sc) in enumerate(((s1_sc, i1_sc), (s2_sc, i2_sc))):
            col = (hd * 2 + half) * LANE
            qh = qp[:, col:col + LANE].astype(BF16)
            sc = lax.dot_general(keys_ref[hd * 2 + half], qh, _NT, preferred_element_type=F32)
            _topk_rows(sc, k, v_sc, i_sc)
        r0 = 0
        for a, n in _candidate_rows(k):
            cand_sc[r0:r0 + n, :] = s1_sc[a:a + 1, :] + s2_sc[0:n, :]
            cidx_sc[r0:r0 + n, :] = i1_sc[a:a + 1, :] * float(N_SUBKEYS) + i2_sc[0:n, :]
            r0 += n
        _topk_rows(cand_sc[...], k, top_sc, e_sc, payload=cidx_sc[...])
        top = top_sc[...]
        ex = jnp.exp(top - top[0:1, :])
        gateT_sc[hd * k:(hd + 1) * k, :] = ex / jnp.sum(ex, axis=0, keepdims=True)
        idxT_sc[hd * k:(hd + 1) * k, :] = (e_sc[...] * float(ROWS_PER_EXPERT)).astype(I32)
    idx_ref[...] = idxT_sc[...].T
    gate_ref[...] = gateT_sc[...].T


def _peer_route(x1, sc2, sh2, w_pq, keys):
    b, s, d = x1.shape
    tt = TT_ROUTE
    k = PEER_TOPK
    nq = w_pq.shape[1]
    per = s // tt
    ncand = sum(n for _, n in _candidate_rows(k))
    ncand_pad = -(-ncand // SUBLANE) * SUBLANE
    full = lambda shape: pl.BlockSpec(shape, lambda bi, i: (0,) * len(shape))
    vec = pl.BlockSpec((1, 1, d), lambda bi, i: (bi, 0, 0))
    out = pl.BlockSpec((tt, PEER_HK), lambda bi, i: (bi * per + i, 0))
    return pl.pallas_call(
        _route_body,
        grid=(b, per),
        in_specs=[pl.BlockSpec((1, tt, d), lambda bi, i: (bi, i, 0)), vec, vec, full((d, nq)),
                  full(keys.shape)],
        out_specs=[out, out],
        out_shape=[jax.ShapeDtypeStruct((b * s, PEER_HK), I32), jax.ShapeDtypeStruct((b * s, PEER_HK), F32)],
        scratch_shapes=[pltpu.VMEM((k, tt), F32)] * 4
                       + [pltpu.VMEM((ncand_pad, tt), F32)] * 2
                       + [pltpu.VMEM((k, tt), F32)] * 2
                       + [pltpu.VMEM((PEER_HK, tt), I32), pltpu.VMEM((PEER_HK, tt), F32)],
        compiler_params=_cparams(("arbitrary", "arbitrary")),
        name="peer_route",
    )(x1, sc2.reshape(b, 1, d), sh2.reshape(b, 1, d), w_pq, keys)


def _pack_table(tab):
    n, d = tab.shape
    t16 = lax.bitcast_convert_type(tab.astype(BF16), jnp.uint16).astype(jnp.uint32)
    words = (t16[:, :d // 2] << 16) | t16[:, d // 2:]
    return lax.bitcast_convert_type(words, I32).reshape(n * ROWS_PER_EXPERT, LANE)


def _unpack(words):
    hi = lax.bitcast_convert_type(words & jnp.int32(-65536), F32)
    lo = lax.bitcast_convert_type(words << 16, F32)
    return hi, lo


def _load_table(tab_hbm, tab_vmem, sem):
    @pl.when(pl.program_id(0) == 0)
    def _():
        cp = pltpu.make_async_copy(tab_hbm, tab_vmem, sem)
        cp.start()
        cp.wait()


def _gather_rows(idx_ref, tab_vmem, t, gbuf):
    for g in range(PEER_HK // GATHER_GROUP):
        idx_g = idx_ref.at[pl.ds(t * PEER_HK + g * GATHER_GROUP, GATHER_GROUP)]
        for k in range(GATHER_GROUP):
            j = g * GATHER_GROUP + k
            r = pl.multiple_of(idx_g[k], ROWS_PER_EXPERT)
            gbuf[j * ROWS_PER_EXPERT:(j + 1) * ROWS_PER_EXPERT, :] = tab_vmem[pl.ds(r, ROWS_PER_EXPERT), :]


def _pipelined_tokens(tn, gather, compute, bufs):
    p = len(bufs) // 2
    set_a, set_b = bufs[:p], bufs[p:]
    for k in range(p):
        gather(k, set_a[k])

    def body(i, carry):
        t = 2 * p * i
        for k in range(p):
            gather(t + p + k, set_b[k])
            compute(t + k, set_a[k])
        for k in range(p):
            gather(jnp.minimum(t + 2 * p + k, tn - 1), set_a[k])
            compute(t + p + k, set_b[k])
        return carry

    lax.fori_loop(0, tn // (2 * p), body, 0)


def _peer_u_body(idx_ref, x_ref, sc_ref, sh_ref, gate_ref, tab_hbm, w_ref, tab_vmem, sem, hs, *bufs, tn):
    _load_table(tab_hbm, tab_vmem, sem)
    scale = 1.0 + sc_ref[0]
    shift = sh_ref[0]

    nchunk = 2 * ROWS_PER_EXPERT
    for c in range(nchunk):
        lanes = slice(c * LANE, (c + 1) * LANE)
        hs[pl.ds(c, tn, stride=nchunk), :] = x_ref[:, lanes] * scale[:, lanes] + shift[:, lanes]

    def compute(t, gbuf):
        h = hs[pl.ds(pl.multiple_of(t * nchunk, nchunk), nchunk), :]
        acc = jnp.zeros((PEER_HK, LANE), F32)
        for c in range(ROWS_PER_EXPERT):
            hi, lo = _unpack(gbuf[pl.ds(c, PEER_HK, stride=ROWS_PER_EXPERT), :])
            acc = acc + hi * h[c:c + 1, :] + lo * h[ROWS_PER_EXPERT + c:ROWS_PER_EXPERT + c + 1, :]
        srow = jnp.sum(acc.T, axis=0, keepdims=True)
        act = 0.5 * srow * (1.0 + lax.erf(srow * (2.0 ** -0.5)))
        w_ref[pl.ds(t, 1), :] = gate_ref[pl.ds(t, 1), :] * act

    _pipelined_tokens(tn, functools.partial(_gather_rows, idx_ref, tab_vmem), compute, bufs)


def _peer_v_body(idx_ref, w_ref, tab_hbm, y_ref, tab_vmem, sem, *bufs, tn):
    _load_table(tab_hbm, tab_vmem, sem)

    def compute(t, gbuf):
        wrow = w_ref[pl.ds(t, 1), :]
        wrep = jnp.broadcast_to(wrow, (LANE, PEER_HK)).T
        his, los = [], []
        for c in range(ROWS_PER_EXPERT):
            hi, lo = _unpack(gbuf[pl.ds(c, PEER_HK, stride=ROWS_PER_EXPERT), :])
            his.append(jnp.sum(wrep * hi, axis=0, keepdims=True))
            los.append(jnp.sum(wrep * lo, axis=0, keepdims=True))
        y_ref[pl.ds(t, 1), :] = jnp.concatenate(his + los, axis=1)

    _pipelined_tokens(tn, functools.partial(_gather_rows, idx_ref, tab_vmem), compute, bufs)


def _peer_experts(idx, gate, x1, sc2, sh2, u_pack, v_pack):
    b, s, d = x1.shape
    t = b * s
    tn = TN_PEER
    per = s // tn
    assert d == 2 * ROWS_PER_EXPERT * LANE
    idx_flat = idx.reshape(t * PEER_HK)
    idx_spec = pl.BlockSpec((tn * PEER_HK,), lambda i: (i,), memory_space=pltpu.SMEM)
    tokrows = pl.BlockSpec((tn, d), lambda i: (i, 0))
    tokhk = pl.BlockSpec((tn, PEER_HK), lambda i: (i, 0))
    vec = pl.BlockSpec((1, 1, d), lambda i: (i // per, 0, 0))
    table = pl.BlockSpec(memory_space=pl.ANY)
    gbuf = pltpu.VMEM((PEER_HK * ROWS_PER_EXPERT, LANE), I32)
    scratch = [pltpu.VMEM(u_pack.shape, I32), pltpu.SemaphoreType.DMA(())]
    gbufs = [gbuf] * (2 * PEER_PIPE)
    w = pl.pallas_call(
        functools.partial(_peer_u_body, tn=tn),
        grid=(t // tn,),
        in_specs=[idx_spec, tokrows, vec, vec, tokhk, table],
        out_specs=tokhk,
        out_shape=jax.ShapeDtypeStruct((t, PEER_HK), F32),
        scratch_shapes=scratch + [pltpu.VMEM((tn * d // LANE, LANE), F32)] + gbufs,
        compiler_params=_cparams(("arbitrary",), VMEM_LIMIT_TABLE),
        name="peer_expert_in",
    )(idx_flat, x1.reshape(t, d), sc2.reshape(b, 1, d), sh2.reshape(b, 1, d), gate, u_pack)
    y = pl.pallas_call(
        functools.partial(_peer_v_body, tn=tn),
        grid=(t // tn,),
        in_specs=[idx_spec, tokhk, table],
        out_specs=tokrows,
        out_shape=jax.ShapeDtypeStruct((t, d), F32),
        scratch_shapes=scratch + gbufs,
        compiler_params=_cparams(("arbitrary",), VMEM_LIMIT_TABLE),
        name="peer_expert_out",
    )(idx_flat, w, v_pack)
    return y.reshape(b, s, d)


def _resnorm_body(x_ref, y_ref, gt_ref, g_ref, b_ref, o_ref, *, alpha):
    v = alpha * x_ref[0] + (1.0 + gt_ref[0]) * y_ref[0]
    o_ref[0] = _layer_norm(v, g_ref[...], b_ref[...])


def _residual_norm(x, y, gt, g_ln, b_ln, alpha):
    b, s, d = x.shape
    tq = TQ_PROJ
    tok = pl.BlockSpec((1, tq, d), lambda bi, i: (bi, i, 0))
    full = pl.BlockSpec((1, d), lambda bi, i: (0, 0))
    return pl.pallas_call(
        functools.partial(_resnorm_body, alpha=alpha),
        grid=(b, s // tq),
        in_specs=[tok, tok, pl.BlockSpec((1, 1, d), lambda bi, i: (bi, 0, 0)), full, full],
        out_specs=tok,
        out_shape=jax.ShapeDtypeStruct((b, s, d), F32),
        compiler_params=_cparams(("arbitrary", "arbitrary")),
        name="residual_norm",
    )(x, y, gt.reshape(b, 1, d), g_ln.reshape(1, d), b_ln.reshape(1, d))


def kernel(x, c, w_ada, b_ada, w_in, g_q, w_uq, g_kv, w_ukv, w_dw, b_dw, g_conv, b_conv, w_pw, b_pw, sinks,
           w_o, g_ln1, b_ln1, w_pq, sub_keys, u_tab, v_tab, g_ln2, b_ln2):
    depth = w_ada.shape[0]
    b, s, d = x.shape
    alpha = float((2 * depth) ** 0.25)
    cos_t, sin_t = _rope_tables(s)
    mod = _ada_mod(c, w_ada, b_ada)
    for l in range(depth):
        sh1, sc1, gt1, sh2, sc2, gt2 = [mod[l, :, i * d:(i + 1) * d] for i in range(6)]
        w_in_p, w_q2, w_kv2 = _prep_inproj_weights(w_in[l], w_uq[l], w_ukv[l])
        q, k, v, g, qs, ks, vs = _in_projection(x, sc1, sh1, w_in_p, g_q[l], w_q2, g_kv[l], w_kv2, cos_t, sin_t)
        o_mla = _mla_attention(q, k, v)
        o_swa = _swa_attention(qs, ks, vs, sinks[l])
        o_conv = _conformer_conv(g, w_dw[l], b_dw[l], g_conv[l], b_conv[l], w_pw[l], b_pw[l])
        wa, wb, wc = _prep_outproj_weights(w_o[l])
        x = _out_projection(o_mla, o_conv, o_swa, x, gt1, wa, wb, wc, g_ln1[l], b_ln1[l], alpha)

        keys = sub_keys[l].reshape(PEER_HEADS * 2, N_SUBKEYS, -1).astype(BF16)
        idx, gate = _peer_route(x, sc2, sh2, w_pq[l].astype(BF16), keys)
        y = _peer_experts(idx, gate, x, sc2, sh2, _pack_table(u_tab[l]), _pack_table(v_tab[l]))
        x = _residual_norm(x, y, gt2, g_ln2[l], b_ln2[l], alpha)
    return x
```

```python
import functools

import jax
import jax.numpy as jnp
from jax import lax
from jax.experimental import pallas as pl
from jax.experimental.pallas import tpu as pltpu

F32 = jnp.float32
BF16 = jnp.bfloat16
I32 = jnp.int32

LANE = 128
SUBLANE = 8
VMEM_BYTES_V7X = 64 * 1024 * 1024
VMEM_LIMIT = 48 * 1024 * 1024
VMEM_LIMIT_TABLE = 56 * 1024 * 1024

MLA_HEADS = 8
MLA_NOPE = 64
MLA_ROPE = 32
MLA_V = 64
MLA_Q_RANK = 256
MLA_KV_RANK = 128
ROPE_BASE = 10000.0
CONV_CH = 256
CONV_WIDTH = 31
SWA_HEADS = 4
SWA_KV_HEADS = 2
SWA_HEAD_DIM = 64
BLOCK = 128
IN_SPLITS = (MLA_Q_RANK, MLA_KV_RANK, MLA_ROPE, 2 * CONV_CH,
             SWA_HEADS * SWA_HEAD_DIM, SWA_KV_HEADS * SWA_HEAD_DIM, SWA_KV_HEADS * SWA_HEAD_DIM)
N_SUBKEYS = 128
PEER_HEADS = 8
PEER_TOPK = 16
PEER_HK = PEER_HEADS * PEER_TOPK
EPS = 1e-6

TQ_PROJ = 512
TQ_ATT = 512
MLA_ROW_SPLIT = 2
TS_SWA = 512
TS_CONV = 512
CONV_HALO = 32
TT_ROUTE = 256
TN_PEER = 256
ROWS_PER_EXPERT = 4
GATHER_GROUP = 8
PEER_PIPE = 2

_NT = (((1,), (1,)), ((), ()))


def _cparams(sem, limit=VMEM_LIMIT):
    return pltpu.CompilerParams(dimension_semantics=sem, vmem_limit_bytes=limit)


def _layer_norm(v, g, b):
    mu = jnp.mean(v, axis=-1, keepdims=True)
    d = v - mu
    var = jnp.mean(d * d, axis=-1, keepdims=True)
    return d * lax.rsqrt(var + EPS) * g + b


def _rms_norm(v, g):
    return v * lax.rsqrt(jnp.mean(v * v, axis=-1, keepdims=True) + EPS) * g


def _ada_body(c_ref, w_ref, b_ref, o_ref):
    c = c_ref[...]
    cs = c * jax.nn.sigmoid(c)
    o_ref[0] = jnp.dot(cs, w_ref[0], preferred_element_type=F32,
                       precision=lax.Precision.HIGHEST) + b_ref[0]


def _ada_mod(c, w_ada, b_ada):
    depth, d, d6 = w_ada.shape
    bp = -(-c.shape[0] // SUBLANE) * SUBLANE
    cp = jnp.pad(c, ((0, bp - c.shape[0]), (0, 0)))
    tn = d
    out = pl.pallas_call(
        _ada_body,
        grid=(depth, d6 // tn),
        in_specs=[pl.BlockSpec((bp, d), lambda l, j: (0, 0)),
                  pl.BlockSpec((1, d, tn), lambda l, j: (l, 0, j)),
                  pl.BlockSpec((1, 1, tn), lambda l, j: (l, 0, j))],
        out_specs=pl.BlockSpec((1, bp, tn), lambda l, j: (l, 0, j)),
        out_shape=jax.ShapeDtypeStruct((depth, bp, d6), F32),
        compiler_params=_cparams(("arbitrary", "arbitrary")),
        name="ada_mod",
    )(cp, w_ada, b_ada.reshape(depth, 1, d6))
    return out[:, :c.shape[0]]


def _inproj_body(x_ref, sc_ref, sh_ref, win_ref, gq_ref, wq_ref, gkv_ref, wkv_ref, cos_ref, sin_ref,
                 q_ref, k_ref, v_ref, g_ref, qs_ref, ks_ref, vs_ref, *, mla_scale, swa_scale):
    hp = MLA_HEADS * LANE
    h = x_ref[0] * (1.0 + sc_ref[0]) + sh_ref[0]
    proj = jnp.dot(h.astype(BF16), win_ref[...], preferred_element_type=F32)
    o = 0
    c_q = proj[:, o:o + MLA_Q_RANK]; o += MLA_Q_RANK
    c_kv = proj[:, o:o + MLA_KV_RANK]; o += MLA_KV_RANK
    kr = proj[:, o:o + LANE]; o += LANE
    kr_sw = proj[:, o:o + LANE]; o += LANE
    a = proj[:, o:o + CONV_CH]; o += CONV_CH
    gate = proj[:, o:o + CONV_CH]; o += CONV_CH
    qs = proj[:, o:o + SWA_HEADS * LANE]; o += SWA_HEADS * LANE
    ks = proj[:, o:o + SWA_KV_HEADS * LANE]; o += SWA_KV_HEADS * LANE
    vs = proj[:, o:o + SWA_KV_HEADS * LANE]

    cos = cos_ref[...]
    sin = sin_ref[...]
    nq = _rms_norm(c_q, gq_ref[...]).astype(BF16)
    q2 = jnp.dot(nq, wq_ref[...], preferred_element_type=F32)
    nkv = _rms_norm(c_kv, gkv_ref[...]).astype(BF16)
    kv2 = jnp.dot(nkv, wkv_ref[...], preferred_element_type=F32)
    k_rope = kr * cos + kr_sw * sin
    for hd in range(MLA_HEADS):
        sl = slice(hd * LANE, (hd + 1) * LANE)
        sl2 = slice(hp + hd * LANE, hp + (hd + 1) * LANE)
        q_ref[0, :, sl] = ((q2[:, sl] * cos + q2[:, sl2] * sin) * mla_scale).astype(BF16)
        k_ref[0, :, sl] = (kv2[:, sl] + k_rope).astype(BF16)
    lane = lax.broadcasted_iota(I32, (1, hp), 1)
    ones_col = jnp.where((lane & (LANE - 1)) == MLA_V, 1.0, 0.0)
    v_ref[0] = (kv2[:, hp:] + ones_col).astype(BF16)
    g_ref[0] = a * jax.nn.sigmoid(gate)
    qs_ref[0] = (qs * swa_scale).astype(BF16)
    ks_ref[0] = ks.astype(BF16)
    vs_ref[0] = vs.astype(BF16)


def _pad_heads(w, heads, width):
    r = w.shape[0]
    w = w.reshape(r, heads, width)
    return jnp.pad(w, ((0, 0), (0, 0), (0, LANE - width))).reshape(r, heads * LANE)


def _rope_place(r1, r2):
    z64 = jnp.zeros(r1.shape[:-1] + (MLA_NOPE,), r1.dtype)
    z32 = jnp.zeros(r1.shape[:-1] + (LANE - MLA_NOPE - MLA_ROPE,), r1.dtype)
    return jnp.concatenate([z64, r1, r2, z32], axis=-1)


def _prep_inproj_weights(w_in, w_uq, w_ukv):
    d = w_in.shape[0]
    half = MLA_ROPE // 2
    offs = [0]
    for s in IN_SPLITS:
        offs.append(offs[-1] + s)
    c_q, c_kv, k_r, conv, q_s, k_s, v_s = [w_in[:, offs[i]:offs[i + 1]] for i in range(7)]
    kr1, kr2 = k_r[:, :half], k_r[:, half:]
    w_in_p = jnp.concatenate([
        c_q, c_kv, _rope_place(kr1, kr2), _rope_place(-kr2, kr1), conv,
        _pad_heads(q_s, SWA_HEADS, SWA_HEAD_DIM), _pad_heads(k_s, SWA_KV_HEADS, SWA_HEAD_DIM),
        _pad_heads(v_s, SWA_KV_HEADS, SWA_HEAD_DIM)], axis=1).astype(BF16)

    wq = w_uq.reshape(MLA_Q_RANK, MLA_HEADS, MLA_NOPE + MLA_ROPE)
    nope, r1, r2 = wq[..., :MLA_NOPE], wq[..., MLA_NOPE:MLA_NOPE + half], wq[..., MLA_NOPE + half:]
    zpad = jnp.zeros(nope.shape[:-1] + (LANE - MLA_NOPE - MLA_ROPE,), nope.dtype)
    placed = jnp.concatenate([nope, r1, r2, zpad], axis=-1).reshape(MLA_Q_RANK, MLA_HEADS * LANE)
    partner = _rope_place(-r2, r1).reshape(MLA_Q_RANK, MLA_HEADS * LANE)
    w_q2 = jnp.concatenate([placed, partner], axis=1).astype(BF16)

    wkv = w_ukv.reshape(MLA_KV_RANK, MLA_HEADS, MLA_NOPE + MLA_V)
    kn = jnp.pad(wkv[..., :MLA_NOPE], ((0, 0), (0, 0), (0, LANE - MLA_NOPE)))
    vv = jnp.pad(wkv[..., MLA_NOPE:], ((0, 0), (0, 0), (0, LANE - MLA_V)))
    w_kv2 = jnp.concatenate([kn.reshape(MLA_KV_RANK, -1), vv.reshape(MLA_KV_RANK, -1)], axis=1).astype(BF16)
    return w_in_p, w_q2, w_kv2


def _rope_tables(seq):
    half = MLA_ROPE // 2
    inv = 1.0 / (ROPE_BASE ** (jnp.arange(0, MLA_ROPE, 2, dtype=F32) / MLA_ROPE))
    ang = jnp.arange(seq, dtype=F32)[:, None] * inv[None, :]
    cos, sin = jnp.cos(ang), jnp.sin(ang)
    ones = jnp.ones((seq, MLA_NOPE), F32)
    zpad = jnp.zeros((seq, LANE - MLA_NOPE - MLA_ROPE), F32)
    cos_t = jnp.concatenate([ones, cos, cos, zpad], axis=1)
    sin_t = jnp.concatenate([jnp.zeros((seq, MLA_NOPE), F32), sin, sin, zpad], axis=1)
    assert half * 2 == MLA_ROPE
    return cos_t, sin_t


def _in_projection(x, sc1, sh1, w_in_p, g_q, w_q2, g_kv, w_kv2, cos_t, sin_t):
    b, s, d = x.shape
    tq = TQ_PROJ
    ncol = w_in_p.shape[1]
    hp = MLA_HEADS * LANE
    full = lambda shape: pl.BlockSpec(shape, lambda bi, i: (0,) * len(shape))
    tok = lambda w: pl.BlockSpec((1, tq, w), lambda bi, i: (bi, i, 0))
    vec = pl.BlockSpec((1, 1, d), lambda bi, i: (bi, 0, 0))
    body = functools.partial(_inproj_body, mla_scale=float((MLA_NOPE + MLA_ROPE) ** -0.5),
                             swa_scale=float(SWA_HEAD_DIM ** -0.5))
    return pl.pallas_call(
        body,
        grid=(b, s // tq),
        in_specs=[tok(d), vec, vec, full((d, ncol)), full((1, MLA_Q_RANK)), full((MLA_Q_RANK, 2 * hp)),
                  full((1, MLA_KV_RANK)), full((MLA_KV_RANK, 2 * hp)),
                  pl.BlockSpec((tq, LANE), lambda bi, i: (i, 0)),
                  pl.BlockSpec((tq, LANE), lambda bi, i: (i, 0))],
        out_specs=[tok(hp), tok(hp), tok(hp), tok(CONV_CH), tok(SWA_HEADS * LANE),
                   tok(SWA_KV_HEADS * LANE), tok(SWA_KV_HEADS * LANE)],
        out_shape=[jax.ShapeDtypeStruct((b, s, hp), BF16), jax.ShapeDtypeStruct((b, s, hp), BF16),
                   jax.ShapeDtypeStruct((b, s, hp), BF16), jax.ShapeDtypeStruct((b, s, CONV_CH), F32),
                   jax.ShapeDtypeStruct((b, s, SWA_HEADS * LANE), BF16),
                   jax.ShapeDtypeStruct((b, s, SWA_KV_HEADS * LANE), BF16),
                   jax.ShapeDtypeStruct((b, s, SWA_KV_HEADS * LANE), BF16)],
        compiler_params=_cparams(("arbitrary", "arbitrary")),
        name="in_projection",
    )(x, sc1.reshape(b, 1, d), sh1.reshape(b, 1, d), w_in_p, g_q.reshape(1, -1), w_q2,
      g_kv.reshape(1, -1), w_kv2, cos_t, sin_t)


def _fold_lane_tiles(v, op):
    out = v[:, 0:LANE]
    for c in range(1, v.shape[1] // LANE):
        out = op(out, v[:, c * LANE:(c + 1) * LANE])
    return out


def _mla_body(q_ref, k_ref, v_ref, o_ref, m_sc, acc_sc, s_a, s_b, *, tq):
    qi = pl.program_id(2)
    m_sc[...] = jnp.full(m_sc.shape, -jnp.inf, F32)
    acc_sc[...] = jnp.zeros(acc_sc.shape, F32)
    tr = tq // MLA_ROW_SPLIT

    def scores(j, s_ref):
        k = k_ref[0, pl.ds(pl.multiple_of(j * tq, tq), tq), :]
        s_ref[...] = lax.dot_general(q_ref[0], k, _NT, preferred_element_type=F32)

    def accumulate(s_ref, j, diagonal):
        v = v_ref[0, pl.ds(pl.multiple_of(j * tq, tq), tq), :]
        for r in range(MLA_ROW_SPLIT):
            rows = slice(r * tr, (r + 1) * tr)
            nk = (r + 1) * tr if diagonal else tq
            s = s_ref[rows, 0:nk]
            if diagonal:
                row = lax.broadcasted_iota(I32, s.shape, 0) + r * tr
                col = lax.broadcasted_iota(I32, s.shape, 1)
                s = jnp.where(col <= row, s, -jnp.inf)
            m_prev = m_sc[rows, :]
            m_new = jnp.maximum(m_prev, jnp.max(_fold_lane_tiles(s, jnp.maximum), axis=1, keepdims=True))
            alpha = jnp.exp(m_prev - m_new)
            p = jnp.exp((s - jnp.concatenate([m_new] * (nk // LANE), axis=1)).astype(BF16))
            acc_sc[rows, :] = alpha * acc_sc[rows, :] + jnp.dot(p, v[0:nk], preferred_element_type=F32)
            m_sc[rows, :] = m_new

    scores(0, s_a)

    def pair(i, carry):
        j = 2 * i
        scores(j + 1, s_b)
        accumulate(s_a, j, False)
        scores(j + 2, s_a)
        accumulate(s_b, j + 1, False)
        return carry

    lax.fori_loop(0, lax.shift_right_logical(qi, 1), pair, 0)

    @pl.when((qi & 1) == 0)
    def _():
        accumulate(s_a, qi, True)

    @pl.when((qi & 1) == 1)
    def _():
        scores(qi, s_b)
        accumulate(s_a, qi - 1, False)
        accumulate(s_b, qi, True)

    acc = acc_sc[...]
    o_ref[0] = (acc / acc[:, MLA_V:MLA_V + 1]).astype(BF16)


def _mla_attention(q, k, v):
    b, s, hp = q.shape
    tq = TQ_ATT
    return pl.pallas_call(
        functools.partial(_mla_body, tq=tq),
        grid=(b, MLA_HEADS, s // tq),
        in_specs=[pl.BlockSpec((1, tq, LANE), lambda bi, h, i: (bi, i, h)),
                  pl.BlockSpec((1, s, LANE), lambda bi, h, i: (bi, 0, h)),
                  pl.BlockSpec((1, s, LANE), lambda bi, h, i: (bi, 0, h))],
        out_specs=pl.BlockSpec((1, tq, LANE), lambda bi, h, i: (bi, i, h)),
        out_shape=jax.ShapeDtypeStruct((b, s, hp), BF16),
        scratch_shapes=[pltpu.VMEM((tq, LANE), F32), pltpu.VMEM((tq, LANE), F32),
                        pltpu.VMEM((tq, tq), F32), pltpu.VMEM((tq, tq), F32)],
        compiler_params=_cparams(("arbitrary", "arbitrary", "arbitrary")),
        name="mla_attention",
    )(q, k, v)


def _swa_body(sink_ref, q_ref, k_ref, kh_ref, v_ref, vh_ref, o_ref, *, nsub):
    i = pl.program_id(1)
    row = lax.broadcasted_iota(I32, (BLOCK, BLOCK), 0)
    col = lax.broadcasted_iota(I32, (BLOCK, BLOCK), 1)
    cur_ok = col <= row
    prev_band = col > row
    group = SWA_HEADS // SWA_KV_HEADS
    stack = lambda parts: jnp.concatenate(parts, axis=0)
    cur_ok = stack([cur_ok] * group)
    prev_band = stack([prev_band] * group)

    work = []
    for j in range(nsub):
        rows = slice(j * BLOCK, (j + 1) * BLOCK)
        prev_ok = jnp.logical_and(prev_band, i > 0) if j == 0 else prev_band
        for g in range(SWA_KV_HEADS):
            lanes = slice(g * LANE, (g + 1) * LANE)
            if j == 0:
                kp, vp = kh_ref[0, :, lanes], vh_ref[0, :, lanes]
            else:
                prows = slice((j - 1) * BLOCK, j * BLOCK)
                kp, vp = k_ref[0, prows, lanes], v_ref[0, prows, lanes]
            kc, vc = k_ref[0, rows, lanes], v_ref[0, rows, lanes]
            heads = [g * group + a for a in range(group)]
            qq = stack([q_ref[0, rows, hq * LANE:(hq + 1) * LANE] for hq in heads])
            sp = lax.dot_general(qq, kp, _NT, preferred_element_type=F32)
            sc = lax.dot_general(qq, kc, _NT, preferred_element_type=F32)
            sink = stack([jnp.full((BLOCK, 1), sink_ref[hq], F32) for hq in heads])
            work.append((rows, heads, prev_ok, sp, sc, sink, vp, vc))

    soft = []
    for rows, heads, prev_ok, sp, sc, sink, vp, vc in work:
        sp = jnp.where(prev_ok, sp, -jnp.inf)
        sc = jnp.where(cur_ok, sc, -jnp.inf)
        m = jnp.maximum(jnp.maximum(jnp.max(sp, axis=1, keepdims=True), jnp.max(sc, axis=1, keepdims=True)), sink)
        pp = jnp.exp(sp - m)
        pc = jnp.exp(sc - m)
        den = jnp.sum(pp, axis=1, keepdims=True) + jnp.sum(pc, axis=1, keepdims=True) + jnp.exp(sink - m)
        soft.append((rows, heads, pp.astype(BF16), pc.astype(BF16), den, vp, vc))

    for rows, heads, pp, pc, den, vp, vc in soft:
        o = (jnp.dot(pp, vp, preferred_element_type=F32) + jnp.dot(pc, vc, preferred_element_type=F32)) / den
        for a, hq in enumerate(heads):
            o_ref[0, rows, hq * LANE:(hq + 1) * LANE] = o[a * BLOCK:(a + 1) * BLOCK, :].astype(BF16)


def _swa_attention(qs, ks, vs, sinks):
    b, s, _ = qs.shape
    ts = TS_SWA
    nsub = ts // BLOCK
    halo = lambda w: pl.BlockSpec((1, BLOCK, w), lambda bi, i: (bi, jnp.maximum(i * nsub - 1, 0), 0))
    tok = lambda w: pl.BlockSpec((1, ts, w), lambda bi, i: (bi, i, 0))
    wq, wk = SWA_HEADS * LANE, SWA_KV_HEADS * LANE
    return pl.pallas_call(
        functools.partial(_swa_body, nsub=nsub),
        grid=(b, s // ts),
        in_specs=[pl.BlockSpec(memory_space=pltpu.SMEM), tok(wq), tok(wk), halo(wk), tok(wk), halo(wk)],
        out_specs=tok(wq),
        out_shape=jax.ShapeDtypeStruct((b, s, wq), BF16),
        compiler_params=_cparams(("arbitrary", "arbitrary")),
        name="swa_attention",
    )(sinks, qs, ks, ks, vs, vs)


def _conv_body(g_ref, gh_ref, wdw_ref, bdw_ref, gln_ref, bln_ref, wpw_ref, bpw_ref, o_ref, buf, shifted, *, ts):
    i = pl.program_id(1)
    halo = gh_ref[0]
    buf[0:CONV_HALO, :] = jnp.where(i > 0, halo, jnp.zeros_like(halo))
    buf[CONV_HALO:CONV_HALO + ts, :] = g_ref[0]
    first = CONV_HALO - (CONV_WIDTH - 1)
    span = shifted.shape[1]
    for r in range(1, SUBLANE):
        shifted[r] = buf[r:r + span, :]
    acc = jnp.zeros((ts, CONV_CH), F32) + bdw_ref[...]
    for j in range(CONV_WIDTH):
        r = (first + j) % SUBLANE
        base = first + j - r
        tap = buf[base:base + ts, :] if r == 0 else shifted[r, base:base + ts, :]
        acc = acc + wdw_ref[j:j + 1, :] * tap
    hn = _layer_norm(acc, gln_ref[...], bln_ref[...])
    hn = hn * jax.nn.sigmoid(hn)
    o = jnp.dot(hn.astype(BF16), wpw_ref[...], preferred_element_type=F32) + bpw_ref[...]
    o_ref[0] = o.astype(BF16)


def _conformer_conv(g, w_dw, b_dw, g_ln, b_ln, w_pw, b_pw):
    b, s, ch = g.shape
    ts = TS_CONV
    per = ts // CONV_HALO
    full = lambda shape: pl.BlockSpec(shape, lambda bi, i: (0,) * len(shape))
    return pl.pallas_call(
        functools.partial(_conv_body, ts=ts),
        grid=(b, s // ts),
        in_specs=[pl.BlockSpec((1, ts, ch), lambda bi, i: (bi, i, 0)),
                  pl.BlockSpec((1, CONV_HALO, ch), lambda bi, i: (bi, jnp.maximum(i * per - 1, 0), 0)),
                  full((CONV_WIDTH, ch)), full((1, ch)), full((1, ch)), full((1, ch)),
                  full((ch, ch)), full((1, ch))],
        out_specs=pl.BlockSpec((1, ts, ch), lambda bi, i: (bi, i, 0)),
        out_shape=jax.ShapeDtypeStruct((b, s, ch), BF16),
        scratch_shapes=[pltpu.VMEM((CONV_HALO + ts, ch), F32),
                        pltpu.VMEM((SUBLANE, CONV_HALO - SUBLANE + ts, ch), F32)],
        compiler_params=_cparams(("arbitrary", "arbitrary")),
        name="conformer_conv",
    )(g, g, w_dw, b_dw.reshape(1, ch), g_ln.reshape(1, ch), b_ln.reshape(1, ch),
      w_pw.astype(BF16), b_pw.reshape(1, ch))


def _outproj_body(om_ref, oc_ref, os_ref, x_ref, gt_ref, wa_ref, wb_ref, wc_ref, g_ref, b_ref, o_ref, *, alpha):
    mix = (jnp.dot(om_ref[0], wa_ref[...], preferred_element_type=F32)
           + jnp.dot(oc_ref[0], wb_ref[...], preferred_element_type=F32)
           + jnp.dot(os_ref[0], wc_ref[...], preferred_element_type=F32))
    y = alpha * x_ref[0] + (1.0 + gt_ref[0]) * mix
    o_ref[0] = _layer_norm(y, g_ref[...], b_ref[...])


def _prep_outproj_weights(w_o):
    d = w_o.shape[1]
    mla_out = MLA_HEADS * MLA_V
    swa_out = SWA_HEADS * SWA_HEAD_DIM
    wa = w_o[:mla_out].reshape(MLA_HEADS, MLA_V, d)
    wa = jnp.pad(wa, ((0, 0), (0, LANE - MLA_V), (0, 0))).reshape(MLA_HEADS * LANE, d)
    wb = w_o[mla_out:mla_out + CONV_CH]
    wc = w_o[mla_out + CONV_CH:].reshape(SWA_HEADS, SWA_HEAD_DIM, d)
    wc = jnp.pad(wc, ((0, 0), (0, LANE - SWA_HEAD_DIM), (0, 0))).reshape(SWA_HEADS * LANE, d)
    assert wc.shape[0] == SWA_HEADS * LANE and swa_out == SWA_HEADS * SWA_HEAD_DIM
    return wa.astype(BF16), wb.astype(BF16), wc.astype(BF16)


def _out_projection(o_mla, o_conv, o_swa, x, gt1, wa, wb, wc, g_ln, b_ln, alpha):
    b, s, d = x.shape
    tq = TQ_PROJ
    full = lambda shape: pl.BlockSpec(shape, lambda bi, i: (0,) * len(shape))
    tok = lambda w: pl.BlockSpec((1, tq, w), lambda bi, i: (bi, i, 0))
    return pl.pallas_call(
        functools.partial(_outproj_body, alpha=alpha),
        grid=(b, s // tq),
        in_specs=[tok(o_mla.shape[2]), tok(o_conv.shape[2]), tok(o_swa.shape[2]), tok(d),
                  pl.BlockSpec((1, 1, d), lambda bi, i: (bi, 0, 0)),
                  full(wa.shape), full(wb.shape), full(wc.shape), full((1, d)), full((1, d))],
        out_specs=tok(d),
        out_shape=jax.ShapeDtypeStruct((b, s, d), F32),
        compiler_params=_cparams(("arbitrary", "arbitrary")),
        name="out_projection",
    )(o_mla, o_conv, o_swa, x, gt1.reshape(b, 1, d), wa, wb, wc, g_ln.reshape(1, d), b_ln.reshape(1, d))


def _topk_rows(s, k, val_sc, idx_sc, payload=None):
    nrows = s.shape[0]
    rid = lax.broadcasted_iota(I32, s.shape, 0).astype(F32)
    for r in range(k):
        m = jnp.max(s, axis=0, keepdims=True)
        sel = jnp.min(jnp.where(s == m, rid, float(nrows)), axis=0, keepdims=True)
        hit = rid == sel
        val_sc[r:r + 1, :] = m
        if payload is None:
            idx_sc[r:r + 1, :] = sel
        else:
            idx_sc[r:r + 1, :] = jnp.max(jnp.where(hit, payload, -1.0), axis=0, keepdims=True)
        s = jnp.where(hit, -jnp.inf, s)


def _candidate_rows(k):
    return [(a, k // (a + 1)) for a in range(k)]


def _route_body(x_ref, sc_ref, sh_ref, wpq_ref, keys_ref, idx_ref, gate_ref,
                s1_sc, i1_sc, s2_sc, i2_sc, cand_sc, cidx_sc, top_sc, e_sc, idxT_sc, gateT_sc):
    k = PEER_TOPK
    ncand = sum(n for _, n in _candidate_rows(k))
    pad = cand_sc.shape[0] - ncand
    cand_sc[ncand:, :] = jnp.full((pad, cand_sc.shape[1]), -jnp.inf, F32)
    cidx_sc[ncand:, :] = jnp.full((pad, cand_sc.shape[1]), -1.0, F32)
    h = x_ref[0] * (1.0 + sc_ref[0]) + sh_ref[0]
    qp = jnp.dot(h.astype(BF16), wpq_ref[...], preferred_element_type=F32)
    for hd in range(PEER_HEADS):
        for half, (v_sc, i_sc) in enumerate(((s1_sc, i1_sc), (s2_sc, i2_sc))):
            col = (hd * 2 + half) * LANE
            qh = qp[:, col:col + LANE].astype(BF16)
            sc = lax.dot_general(keys_ref[hd * 2 + half], qh, _NT, preferred_element_type=F32)
            _topk_rows(sc, k, v_sc, i_sc)
        r0 = 0
        for a, n in _candidate_rows(k):
            cand_sc[r0:r0 + n, :] = s1_sc[a:a + 1, :] + s2_sc[0:n, :]
            cidx_sc[r0:r0 + n, :] = i1_sc[a:a + 1, :] * float(N_SUBKEYS) + i2_sc[0:n, :]
            r0 += n
        _topk_rows(cand_sc[...], k, top_sc, e_sc, payload=cidx_sc[...])
        top = top_sc[...]
        ex = jnp.exp(top - top[0:1, :])
        gateT_sc[hd * k:(hd + 1) * k, :] = ex / jnp.sum(ex, axis=0, keepdims=True)
        idxT_sc[hd * k:(hd + 1) * k, :] = (e_sc[...] * float(ROWS_PER_EXPERT)).astype(I32)
    idx_ref[...] = idxT_sc[...].T
    gate_ref[...] = gateT_sc[...].T


def _peer_route(x1, sc2, sh2, w_pq, keys):
    b, s, d = x1.shape
    tt = TT_ROUTE
    k = PEER_TOPK
    nq = w_pq.shape[1]
    per = s // tt
    ncand = sum(n for _, n in _candidate_rows(k))
    ncand_pad = -(-ncand // SUBLANE) * SUBLANE
    full = lambda shape: pl.BlockSpec(shape, lambda bi, i: (0,) * len(shape))
    vec = pl.BlockSpec((1, 1, d), lambda bi, i: (bi, 0, 0))
    out = pl.BlockSpec((tt, PEER_HK), lambda bi, i: (bi * per + i, 0))
    return pl.pallas_call(
        _route_body,
        grid=(b, per),
        in_specs=[pl.BlockSpec((1, tt, d), lambda bi, i: (bi, i, 0)), vec, vec, full((d, nq)),
                  full(keys.shape)],
        out_specs=[out, out],
        out_shape=[jax.ShapeDtypeStruct((b * s, PEER_HK), I32), jax.ShapeDtypeStruct((b * s, PEER_HK), F32)],
        scratch_shapes=[pltpu.VMEM((k, tt), F32)] * 4
                       + [pltpu.VMEM((ncand_pad, tt), F32)] * 2
                       + [pltpu.VMEM((k, tt), F32)] * 2
                       + [pltpu.VMEM((PEER_HK, tt), I32), pltpu.VMEM((PEER_HK, tt), F32)],
        compiler_params=_cparams(("arbitrary", "arbitrary")),
        name="peer_route",
    )(x1, sc2.reshape(b, 1, d), sh2.reshape(b, 1, d), w_pq, keys)


def _pack_table(tab):
    n, d = tab.shape
    t16 = lax.bitcast_convert_type(tab.astype(BF16), jnp.uint16).astype(jnp.uint32)
    words = (t16[:, :d // 2] << 16) | t16[:, d // 2:]
    return lax.bitcast_convert_type(words, I32).reshape(n * ROWS_PER_EXPERT, LANE)


def _unpack(words):
    hi = lax.bitcast_convert_type(words & jnp.int32(-65536), F32)
    lo = lax.bitcast_convert_type(words << 16, F32)
    return hi, lo


def _load_table(tab_hbm, tab_vmem, sem):
    @pl.when(pl.program_id(0) == 0)
    def _():
        cp = pltpu.make_async_copy(tab_hbm, tab_vmem, sem)
        cp.start()
        cp.wait()


def _gather_rows(idx_ref, tab_vmem, t, gbuf):
    idx_t = idx_ref.at[pl.ds(t * PEER_HK, PEER_HK)]
    for g in range(PEER_HK // GATHER_GROUP):
        idx_g = idx_t.at[pl.ds(g * GATHER_GROUP, GATHER_GROUP)]
        for k in range(GATHER_GROUP):
            j = g * GATHER_GROUP + k
            r = pl.multiple_of(idx_g[k], ROWS_PER_EXPERT)
            gbuf[j * ROWS_PER_EXPERT:(j + 1) * ROWS_PER_EXPERT, :] = tab_vmem[pl.ds(r, ROWS_PER_EXPERT), :]


def _pipelined_tokens(tn, gather, compute, bufs):
    p = len(bufs) // 2
    set_a, set_b = bufs[:p], bufs[p:]
    for k in range(p):
        gather(k, set_a[k])

    def body(i, carry):
        t = 2 * p * i
        for k in range(p):
            gather(t + p + k, set_b[k])
            compute(t + k, set_a[k])
        for k in range(p):
            gather(jnp.minimum(t + 2 * p + k, tn - 1), set_a[k])
            compute(t + p + k, set_b[k])
        return carry

    lax.fori_loop(0, tn // (2 * p), body, 0)


def _peer_u_body(idx_ref, x_ref, sc_ref, sh_ref, gate_ref, tab_hbm, w_ref, tab_vmem, sem, hs, *bufs, tn):
    _load_table(tab_hbm, tab_vmem, sem)
    scale = 1.0 + sc_ref[0]
    shift = sh_ref[0]

    nchunk = 2 * ROWS_PER_EXPERT
    for c in range(nchunk):
        lanes = slice(c * LANE, (c + 1) * LANE)
        hs[pl.ds(c, tn, stride=nchunk), :] = x_ref[:, lanes] * scale[:, lanes] + shift[:, lanes]

    def compute(t, gbuf):
        h = hs[pl.ds(pl.multiple_of(t * nchunk, nchunk), nchunk), :]
        acc = jnp.zeros((PEER_HK, LANE), F32)
        for c in range(ROWS_PER_EXPERT):
            hi, lo = _unpack(gbuf[pl.ds(c, PEER_HK, stride=ROWS_PER_EXPERT), :])
            acc = acc + hi * h[c:c + 1, :] + lo * h[ROWS_PER_EXPERT + c:ROWS_PER_EXPERT + c + 1, :]
        srow = jnp.sum(acc.T, axis=0, keepdims=True)
        act = 0.5 * srow * (1.0 + lax.erf(srow * (2.0 ** -0.5)))
        w_ref[pl.ds(t, 1), :] = gate_ref[pl.ds(t, 1), :] * act

    _pipelined_tokens(tn, functools.partial(_gather_rows, idx_ref, tab_vmem), compute, bufs)


def _peer_v_body(idx_ref, w_ref, tab_hbm, y_ref, tab_vmem, sem, *bufs, tn):
    _load_table(tab_hbm, tab_vmem, sem)

    def compute(t, gbuf):
        wrow = w_ref[pl.ds(t, 1), :]
        wrep = jnp.broadcast_to(wrow, (LANE, PEER_HK)).T
        his, los = [], []
        for c in range(ROWS_PER_EXPERT):
            hi, lo = _unpack(gbuf[pl.ds(c, PEER_HK, stride=ROWS_PER_EXPERT), :])
            his.append(jnp.sum(wrep * hi, axis=0, keepdims=True))
            los.append(jnp.sum(wrep * lo, axis=0, keepdims=True))
        y_ref[pl.ds(t, 1), :] = jnp.concatenate(his + los, axis=1)

    _pipelined_tokens(tn, functools.partial(_gather_rows, idx_ref, tab_vmem), compute, bufs)


def _peer_experts(idx, gate, x1, sc2, sh2, u_pack, v_pack):
    b, s, d = x1.shape
    t = b * s
    tn = TN_PEER
    per = s // tn
    assert d == 2 * ROWS_PER_EXPERT * LANE
    idx_flat = idx.reshape(t * PEER_HK)
    idx_spec = pl.BlockSpec((tn * PEER_HK,), lambda i: (i,), memory_space=pltpu.SMEM)
    tokrows = pl.BlockSpec((tn, d), lambda i: (i, 0))
    tokhk = pl.BlockSpec((tn, PEER_HK), lambda i: (i, 0))
    vec = pl.BlockSpec((1, 1, d), lambda i: (i // per, 0, 0))
    table = pl.BlockSpec(memory_space=pl.ANY)
    gbuf = pltpu.VMEM((PEER_HK * ROWS_PER_EXPERT, LANE), I32)
    scratch = [pltpu.VMEM(u_pack.shape, I32), pltpu.SemaphoreType.DMA(())]
    gbufs = [gbuf] * (2 * PEER_PIPE)
    w = pl.pallas_call(
        functools.partial(_peer_u_body, tn=tn),
        grid=(t // tn,),
        in_specs=[idx_spec, tokrows, vec, vec, tokhk, table],
        out_specs=tokhk,
        out_shape=jax.ShapeDtypeStruct((t, PEER_HK), F32),
        scratch_shapes=scratch + [pltpu.VMEM((tn * d // LANE, LANE), F32)] + gbufs,
        compiler_params=_cparams(("arbitrary",), VMEM_LIMIT_TABLE),
        name="peer_expert_in",
    )(idx_flat, x1.reshape(t, d), sc2.reshape(b, 1, d), sh2.reshape(b, 1, d), gate, u_pack)
    y = pl.pallas_call(
        functools.partial(_peer_v_body, tn=tn),
        grid=(t // tn,),
        in_specs=[idx_spec, tokhk, table],
        out_specs=tokrows,
        out_shape=jax.ShapeDtypeStruct((t, d), F32),
        scratch_shapes=scratch + gbufs,
        compiler_params=_cparams(("arbitrary",), VMEM_LIMIT_TABLE),
        name="peer_expert_out",
    )(idx_flat, w, v_pack)
    return y.reshape(b, s, d)


def _resnorm_body(x_ref, y_ref, gt_ref, g_ref, b_ref, o_ref, *, alpha):
    v = alpha * x_ref[0] + (1.0 + gt_ref[0]) * y_ref[0]
    o_ref[0] = _layer_norm(v, g_ref[...], b_ref[...])


def _residual_norm(x, y, gt, g_ln, b_ln, alpha):
    b, s, d = x.shape
    tq = TQ_PROJ
    tok = pl.BlockSpec((1, tq, d), lambda bi, i: (bi, i, 0))
    full = pl.BlockSpec((1, d), lambda bi, i: (0, 0))
    return pl.pallas_call(
        functools.partial(_resnorm_body, alpha=alpha),
        grid=(b, s // tq),
        in_specs=[tok, tok, pl.BlockSpec((1, 1, d), lambda bi, i: (bi, 0, 0)), full, full],
        out_specs=tok,
        out_shape=jax.ShapeDtypeStruct((b, s, d), F32),
        compiler_params=_cparams(("arbitrary", "arbitrary")),
        name="residual_norm",
    )(x, y, gt.reshape(b, 1, d), g_ln.reshape(1, d), b_ln.reshape(1, d))


def kernel(x, c, w_ada, b_ada, w_in, g_q, w_uq, g_kv, w_ukv, w_dw, b_dw, g_conv, b_conv, w_pw, b_pw, sinks,
           w_o, g_ln1, b_ln1, w_pq, sub_keys, u_tab, v_tab, g_ln2, b_ln2):
    depth = w_ada.shape[0]
    b, s, d = x.shape
    alpha = float((2 * depth) ** 0.25)
    cos_t, sin_t = _rope_tables(s)
    mod = _ada_mod(c, w_ada, b_ada)
    for l in range(depth):
        sh1, sc1, gt1, sh2, sc2, gt2 = [mod[l, :, i * d:(i + 1) * d] for i in range(6)]
        w_in_p, w_q2, w_kv2 = _prep_inproj_weights(w_in[l], w_uq[l], w_ukv[l])
        q, k, v, g, qs, ks, vs = _in_projection(x, sc1, sh1, w_in_p, g_q[l], w_q2, g_kv[l], w_kv2, cos_t, sin_t)
        o_mla = _mla_attention(q, k, v)
        o_swa = _swa_attention(qs, ks, vs, sinks[l])
        o_conv = _conformer_conv(g, w_dw[l], b_dw[l], g_conv[l], b_conv[l], w_pw[l], b_pw[l])
        wa, wb, wc = _prep_outproj_weights(w_o[l])
        x = _out_projection(o_mla, o_conv, o_swa, x, gt1, wa, wb, wc, g_ln1[l], b_ln1[l], alpha)

        keys = sub_keys[l].reshape(PEER_HEADS * 2, N_SUBKEYS, -1).astype(BF16)
        idx, gate = _peer_route(x, sc2, sh2, w_pq[l].astype(BF16), keys)
        y = _peer_experts(idx, gate, x, sc2, sh2, _pack_table(u_tab[l]), _pack_table(v_tab[l]))
        x = _residual_norm(x, y, gt2, g_ln2[l], b_ln2[l], alpha)
    return x
```

```python
import functools

import jax
import jax.numpy as jnp
from jax import lax
from jax.experimental import pallas as pl
from jax.experimental.pallas import tpu as pltpu

F32 = jnp.float32
BF16 = jnp.bfloat16
I32 = jnp.int32

LANE = 128
SUBLANE = 8
VMEM_BYTES_V7X = 64 * 1024 * 1024
VMEM_LIMIT = 48 * 1024 * 1024
VMEM_LIMIT_TABLE = 56 * 1024 * 1024

MLA_HEADS = 8
MLA_NOPE = 64
MLA_ROPE = 32
MLA_V = 64
MLA_Q_RANK = 256
MLA_KV_RANK = 128
ROPE_BASE = 10000.0
CONV_CH = 256
CONV_WIDTH = 31
SWA_HEADS = 4
SWA_KV_HEADS = 2
SWA_HEAD_DIM = 64
BLOCK = 128
IN_SPLITS = (MLA_Q_RANK, MLA_KV_RANK, MLA_ROPE, 2 * CONV_CH,
             SWA_HEADS * SWA_HEAD_DIM, SWA_KV_HEADS * SWA_HEAD_DIM, SWA_KV_HEADS * SWA_HEAD_DIM)
N_SUBKEYS = 128
PEER_HEADS = 8
PEER_TOPK = 16
PEER_HK = PEER_HEADS * PEER_TOPK
EPS = 1e-6

TQ_PROJ = 512
TQ_ATT = 512
MLA_ROW_SPLIT = 2
TS_SWA = 512
TS_CONV = 512
CONV_HALO = 32
TT_ROUTE = 256
TN_PEER = 256
ROWS_PER_EXPERT = 4
GATHER_GROUP = 8
PEER_PIPE_IN = 2
PEER_PIPE_OUT = 4

_NT = (((1,), (1,)), ((), ()))


def _cparams(sem, limit=VMEM_LIMIT):
    return pltpu.CompilerParams(dimension_semantics=sem, vmem_limit_bytes=limit)


def _layer_norm(v, g, b):
    mu = jnp.mean(v, axis=-1, keepdims=True)
    d = v - mu
    var = jnp.mean(d * d, axis=-1, keepdims=True)
    return d * lax.rsqrt(var + EPS) * g + b


def _rms_norm(v, g):
    return v * lax.rsqrt(jnp.mean(v * v, axis=-1, keepdims=True) + EPS) * g


def _ada_body(c_ref, w_ref, b_ref, o_ref):
    c = c_ref[...]
    cs = c * jax.nn.sigmoid(c)
    o_ref[0] = jnp.dot(cs, w_ref[0], preferred_element_type=F32,
                       precision=lax.Precision.HIGHEST) + b_ref[0]


def _ada_mod(c, w_ada, b_ada):
    depth, d, d6 = w_ada.shape
    bp = -(-c.shape[0] // SUBLANE) * SUBLANE
    cp = jnp.pad(c, ((0, bp - c.shape[0]), (0, 0)))
    tn = d
    out = pl.pallas_call(
        _ada_body,
        grid=(depth, d6 // tn),
        in_specs=[pl.BlockSpec((bp, d), lambda l, j: (0, 0)),
                  pl.BlockSpec((1, d, tn), lambda l, j: (l, 0, j)),
                  pl.BlockSpec((1, 1, tn), lambda l, j: (l, 0, j))],
        out_specs=pl.BlockSpec((1, bp, tn), lambda l, j: (l, 0, j)),
        out_shape=jax.ShapeDtypeStruct((depth, bp, d6), F32),
        compiler_params=_cparams(("arbitrary", "arbitrary")),
        name="ada_mod",
    )(cp, w_ada, b_ada.reshape(depth, 1, d6))
    return out[:, :c.shape[0]]


def _inproj_body(x_ref, sc_ref, sh_ref, win_ref, gq_ref, wq_ref, gkv_ref, wkv_ref, cos_ref, sin_ref,
                 q_ref, k_ref, v_ref, g_ref, qs_ref, ks_ref, vs_ref, *, mla_scale, swa_scale):
    hp = MLA_HEADS * LANE
    h = x_ref[0] * (1.0 + sc_ref[0]) + sh_ref[0]
    proj = jnp.dot(h.astype(BF16), win_ref[...], preferred_element_type=F32)
    o = 0
    c_q = proj[:, o:o + MLA_Q_RANK]; o += MLA_Q_RANK
    c_kv = proj[:, o:o + MLA_KV_RANK]; o += MLA_KV_RANK
    kr = proj[:, o:o + LANE]; o += LANE
    kr_sw = proj[:, o:o + LANE]; o += LANE
    a = proj[:, o:o + CONV_CH]; o += CONV_CH
    gate = proj[:, o:o + CONV_CH]; o += CONV_CH
    qs = proj[:, o:o + SWA_HEADS * LANE]; o += SWA_HEADS * LANE
    ks = proj[:, o:o + SWA_KV_HEADS * LANE]; o += SWA_KV_HEADS * LANE
    vs = proj[:, o:o + SWA_KV_HEADS * LANE]

    cos = cos_ref[...]
    sin = sin_ref[...]
    nq = _rms_norm(c_q, gq_ref[...]).astype(BF16)
    q2 = jnp.dot(nq, wq_ref[...], preferred_element_type=F32)
    nkv = _rms_norm(c_kv, gkv_ref[...]).astype(BF16)
    kv2 = jnp.dot(nkv, wkv_ref[...], preferred_element_type=F32)
    k_rope = kr * cos + kr_sw * sin
    for hd in range(MLA_HEADS):
        sl = slice(hd * LANE, (hd + 1) * LANE)
        sl2 = slice(hp + hd * LANE, hp + (hd + 1) * LANE)
        q_ref[0, :, sl] = ((q2[:, sl] * cos + q2[:, sl2] * sin) * mla_scale).astype(BF16)
        k_ref[0, :, sl] = (kv2[:, sl] + k_rope).astype(BF16)
    lane = lax.broadcasted_iota(I32, (1, hp), 1)
    ones_col = jnp.where((lane & (LANE - 1)) == MLA_V, 1.0, 0.0)
    v_ref[0] = (kv2[:, hp:] + ones_col).astype(BF16)
    g_ref[0] = a * jax.nn.sigmoid(gate)
    qs_ref[0] = (qs * swa_scale).astype(BF16)
    ks_ref[0] = ks.astype(BF16)
    vs_ref[0] = vs.astype(BF16)


def _pad_heads(w, heads, width):
    r = w.shape[0]
    w = w.reshape(r, heads, width)
    return jnp.pad(w, ((0, 0), (0, 0), (0, LANE - width))).reshape(r, heads * LANE)


def _rope_place(r1, r2):
    z64 = jnp.zeros(r1.shape[:-1] + (MLA_NOPE,), r1.dtype)
    z32 = jnp.zeros(r1.shape[:-1] + (LANE - MLA_NOPE - MLA_ROPE,), r1.dtype)
    return jnp.concatenate([z64, r1, r2, z32], axis=-1)


def _prep_inproj_weights(w_in, w_uq, w_ukv):
    d = w_in.shape[0]
    half = MLA_ROPE // 2
    offs = [0]
    for s in IN_SPLITS:
        offs.append(offs[-1] + s)
    c_q, c_kv, k_r, conv, q_s, k_s, v_s = [w_in[:, offs[i]:offs[i + 1]] for i in range(7)]
    kr1, kr2 = k_r[:, :half], k_r[:, half:]
    w_in_p = jnp.concatenate([
        c_q, c_kv, _rope_place(kr1, kr2), _rope_place(-kr2, kr1), conv,
        _pad_heads(q_s, SWA_HEADS, SWA_HEAD_DIM), _pad_heads(k_s, SWA_KV_HEADS, SWA_HEAD_DIM),
        _pad_heads(v_s, SWA_KV_HEADS, SWA_HEAD_DIM)], axis=1).astype(BF16)

    wq = w_uq.reshape(MLA_Q_RANK, MLA_HEADS, MLA_NOPE + MLA_ROPE)
    nope, r1, r2 = wq[..., :MLA_NOPE], wq[..., MLA_NOPE:MLA_NOPE + half], wq[..., MLA_NOPE + half:]
    zpad = jnp.zeros(nope.shape[:-1] + (LANE - MLA_NOPE - MLA_ROPE,), nope.dtype)
    placed = jnp.concatenate([nope, r1, r2, zpad], axis=-1).reshape(MLA_Q_RANK, MLA_HEADS * LANE)
    partner = _rope_place(-r2, r1).reshape(MLA_Q_RANK, MLA_HEADS * LANE)
    w_q2 = jnp.concatenate([placed, partner], axis=1).astype(BF16)

    wkv = w_ukv.reshape(MLA_KV_RANK, MLA_HEADS, MLA_NOPE + MLA_V)
    kn = jnp.pad(wkv[..., :MLA_NOPE], ((0, 0), (0, 0), (0, LANE - MLA_NOPE)))
    vv = jnp.pad(wkv[..., MLA_NOPE:], ((0, 0), (0, 0), (0, LANE - MLA_V)))
    w_kv2 = jnp.concatenate([kn.reshape(MLA_KV_RANK, -1), vv.reshape(MLA_KV_RANK, -1)], axis=1).astype(BF16)
    return w_in_p, w_q2, w_kv2


def _rope_tables(seq):
    half = MLA_ROPE // 2
    inv = 1.0 / (ROPE_BASE ** (jnp.arange(0, MLA_ROPE, 2, dtype=F32) / MLA_ROPE))
    ang = jnp.arange(seq, dtype=F32)[:, None] * inv[None, :]
    cos, sin = jnp.cos(ang), jnp.sin(ang)
    ones = jnp.ones((seq, MLA_NOPE), F32)
    zpad = jnp.zeros((seq, LANE - MLA_NOPE - MLA_ROPE), F32)
    cos_t = jnp.concatenate([ones, cos, cos, zpad], axis=1)
    sin_t = jnp.concatenate([jnp.zeros((seq, MLA_NOPE), F32), sin, sin, zpad], axis=1)
    assert half * 2 == MLA_ROPE
    return cos_t, sin_t


def _in_projection(x, sc1, sh1, w_in_p, g_q, w_q2, g_kv, w_kv2, cos_t, sin_t):
    b, s, d = x.shape
    tq = TQ_PROJ
    ncol = w_in_p.shape[1]
    hp = MLA_HEADS * LANE
    full = lambda shape: pl.BlockSpec(shape, lambda bi, i: (0,) * len(shape))
    tok = lambda w: pl.BlockSpec((1, tq, w), lambda bi, i: (bi, i, 0))
    vec = pl.BlockSpec((1, 1, d), lambda bi, i: (bi, 0, 0))
    body = functools.partial(_inproj_body, mla_scale=float((MLA_NOPE + MLA_ROPE) ** -0.5),
                             swa_scale=float(SWA_HEAD_DIM ** -0.5))
    return pl.pallas_call(
        body,
        grid=(b, s // tq),
        in_specs=[tok(d), vec, vec, full((d, ncol)), full((1, MLA_Q_RANK)), full((MLA_Q_RANK, 2 * hp)),
                  full((1, MLA_KV_RANK)), full((MLA_KV_RANK, 2 * hp)),
                  pl.BlockSpec((tq, LANE), lambda bi, i: (i, 0)),
                  pl.BlockSpec((tq, LANE), lambda bi, i: (i, 0))],
        out_specs=[tok(hp), tok(hp), tok(hp), tok(CONV_CH), tok(SWA_HEADS * LANE),
                   tok(SWA_KV_HEADS * LANE), tok(SWA_KV_HEADS * LANE)],
        out_shape=[jax.ShapeDtypeStruct((b, s, hp), BF16), jax.ShapeDtypeStruct((b, s, hp), BF16),
                   jax.ShapeDtypeStruct((b, s, hp), BF16), jax.ShapeDtypeStruct((b, s, CONV_CH), F32),
                   jax.ShapeDtypeStruct((b, s, SWA_HEADS * LANE), BF16),
                   jax.ShapeDtypeStruct((b, s, SWA_KV_HEADS * LANE), BF16),
                   jax.ShapeDtypeStruct((b, s, SWA_KV_HEADS * LANE), BF16)],
        compiler_params=_cparams(("arbitrary", "arbitrary")),
        name="in_projection",
    )(x, sc1.reshape(b, 1, d), sh1.reshape(b, 1, d), w_in_p, g_q.reshape(1, -1), w_q2,
      g_kv.reshape(1, -1), w_kv2, cos_t, sin_t)


def _fold_lane_tiles(v, op):
    out = v[:, 0:LANE]
    for c in range(1, v.shape[1] // LANE):
        out = op(out, v[:, c * LANE:(c + 1) * LANE])
    return out


def _mla_body(q_ref, k_ref, v_ref, o_ref, m_sc, acc_sc, s_a, s_b, *, tq):
    qi = pl.program_id(2)
    m_sc[...] = jnp.full(m_sc.shape, -jnp.inf, F32)
    acc_sc[...] = jnp.zeros(acc_sc.shape, F32)
    tr = tq // MLA_ROW_SPLIT

    def scores(j, s_ref):
        k = k_ref[0, pl.ds(pl.multiple_of(j * tq, tq), tq), :]
        s_ref[...] = lax.dot_general(q_ref[0], k, _NT, preferred_element_type=F32)

    def accumulate(s_ref, j, diagonal):
        v = v_ref[0, pl.ds(pl.multiple_of(j * tq, tq), tq), :]
        for r in range(MLA_ROW_SPLIT):
            rows = slice(r * tr, (r + 1) * tr)
            nk = (r + 1) * tr if diagonal else tq
            s = s_ref[rows, 0:nk]
            if diagonal:
                row = lax.broadcasted_iota(I32, s.shape, 0) + r * tr
                col = lax.broadcasted_iota(I32, s.shape, 1)
                s = jnp.where(col <= row, s, -jnp.inf)
            m_prev = m_sc[rows, :]
            m_new = jnp.maximum(m_prev, jnp.max(_fold_lane_tiles(s, jnp.maximum), axis=1, keepdims=True))
            alpha = jnp.exp(m_prev - m_new)
            p = jnp.exp((s - jnp.concatenate([m_new] * (nk // LANE), axis=1)).astype(BF16))
            acc_sc[rows, :] = alpha * acc_sc[rows, :] + jnp.dot(p, v[0:nk], preferred_element_type=F32)
            m_sc[rows, :] = m_new

    scores(0, s_a)

    def pair(i, carry):
        j = 2 * i
        scores(j + 1, s_b)
        accumulate(s_a, j, False)
        scores(j + 2, s_a)
        accumulate(s_b, j + 1, False)
        return carry

    lax.fori_loop(0, lax.shift_right_logical(qi, 1), pair, 0)

    @pl.when((qi & 1) == 0)
    def _():
        accumulate(s_a, qi, True)

    @pl.when((qi & 1) == 1)
    def _():
        scores(qi, s_b)
        accumulate(s_a, qi - 1, False)
        accumulate(s_b, qi, True)

    acc = acc_sc[...]
    o_ref[0] = (acc / acc[:, MLA_V:MLA_V + 1]).astype(BF16)


def _mla_attention(q, k, v):
    b, s, hp = q.shape
    tq = TQ_ATT
    return pl.pallas_call(
        functools.partial(_mla_body, tq=tq),
        grid=(b, MLA_HEADS, s // tq),
        in_specs=[pl.BlockSpec((1, tq, LANE), lambda bi, h, i: (bi, i, h)),
                  pl.BlockSpec((1, s, LANE), lambda bi, h, i: (bi, 0, h)),
                  pl.BlockSpec((1, s, LANE), lambda bi, h, i: (bi, 0, h))],
        out_specs=pl.BlockSpec((1, tq, LANE), lambda bi, h, i: (bi, i, h)),
        out_shape=jax.ShapeDtypeStruct((b, s, hp), BF16),
        scratch_shapes=[pltpu.VMEM((tq, LANE), F32), pltpu.VMEM((tq, LANE), F32),
                        pltpu.VMEM((tq, tq), F32), pltpu.VMEM((tq, tq), F32)],
        compiler_params=_cparams(("arbitrary", "arbitrary", "arbitrary")),
        name="mla_attention",
    )(q, k, v)


def _swa_body(sink_ref, q_ref, k_ref, kh_ref, v_ref, vh_ref, o_ref, *, nsub):
    i = pl.program_id(1)
    row = lax.broadcasted_iota(I32, (BLOCK, BLOCK), 0)
    col = lax.broadcasted_iota(I32, (BLOCK, BLOCK), 1)
    cur_ok = col <= row
    prev_band = col > row
    group = SWA_HEADS // SWA_KV_HEADS
    stack = lambda parts: jnp.concatenate(parts, axis=0)
    cur_ok = stack([cur_ok] * group)
    prev_band = stack([prev_band] * group)

    work = []
    for j in range(nsub):
        rows = slice(j * BLOCK, (j + 1) * BLOCK)
        prev_ok = jnp.logical_and(prev_band, i > 0) if j == 0 else prev_band
        for g in range(SWA_KV_HEADS):
            lanes = slice(g * LANE, (g + 1) * LANE)
            if j == 0:
                kp, vp = kh_ref[0, :, lanes], vh_ref[0, :, lanes]
            else:
                prows = slice((j - 1) * BLOCK, j * BLOCK)
                kp, vp = k_ref[0, prows, lanes], v_ref[0, prows, lanes]
            kc, vc = k_ref[0, rows, lanes], v_ref[0, rows, lanes]
            heads = [g * group + a for a in range(group)]
            qq = stack([q_ref[0, rows, hq * LANE:(hq + 1) * LANE] for hq in heads])
            sp = lax.dot_general(qq, kp, _NT, preferred_element_type=F32)
            sc = lax.dot_general(qq, kc, _NT, preferred_element_type=F32)
            sink = stack([jnp.full((BLOCK, 1), sink_ref[hq], F32) for hq in heads])
            work.append((rows, heads, prev_ok, sp, sc, sink, vp, vc))

    soft = []
    for rows, heads, prev_ok, sp, sc, sink, vp, vc in work:
        sp = jnp.where(prev_ok, sp, -jnp.inf)
        sc = jnp.where(cur_ok, sc, -jnp.inf)
        m = jnp.maximum(jnp.maximum(jnp.max(sp, axis=1, keepdims=True), jnp.max(sc, axis=1, keepdims=True)), sink)
        pp = jnp.exp(sp - m)
        pc = jnp.exp(sc - m)
        den = jnp.sum(pp, axis=1, keepdims=True) + jnp.sum(pc, axis=1, keepdims=True) + jnp.exp(sink - m)
        soft.append((rows, heads, pp.astype(BF16), pc.astype(BF16), den, vp, vc))

    for rows, heads, pp, pc, den, vp, vc in soft:
        o = (jnp.dot(pp, vp, preferred_element_type=F32) + jnp.dot(pc, vc, preferred_element_type=F32)) / den
        for a, hq in enumerate(heads):
            o_ref[0, rows, hq * LANE:(hq + 1) * LANE] = o[a * BLOCK:(a + 1) * BLOCK, :].astype(BF16)


def _swa_attention(qs, ks, vs, sinks):
    b, s, _ = qs.shape
    ts = TS_SWA
    nsub = ts // BLOCK
    halo = lambda w: pl.BlockSpec((1, BLOCK, w), lambda bi, i: (bi, jnp.maximum(i * nsub - 1, 0), 0))
    tok = lambda w: pl.BlockSpec((1, ts, w), lambda bi, i: (bi, i, 0))
    wq, wk = SWA_HEADS * LANE, SWA_KV_HEADS * LANE
    return pl.pallas_call(
        functools.partial(_swa_body, nsub=nsub),
        grid=(b, s // ts),
        in_specs=[pl.BlockSpec(memory_space=pltpu.SMEM), tok(wq), tok(wk), halo(wk), tok(wk), halo(wk)],
        out_specs=tok(wq),
        out_shape=jax.ShapeDtypeStruct((b, s, wq), BF16),
        compiler_params=_cparams(("arbitrary", "arbitrary")),
        name="swa_attention",
    )(sinks, qs, ks, ks, vs, vs)


def _conv_body(g_ref, gh_ref, wdw_ref, bdw_ref, gln_ref, bln_ref, wpw_ref, bpw_ref, o_ref, buf, shifted, *, ts):
    i = pl.program_id(1)
    halo = gh_ref[0]
    buf[0:CONV_HALO, :] = jnp.where(i > 0, halo, jnp.zeros_like(halo))
    buf[CONV_HALO:CONV_HALO + ts, :] = g_ref[0]
    first = CONV_HALO - (CONV_WIDTH - 1)
    span = shifted.shape[1]
    for r in range(1, SUBLANE):
        shifted[r] = buf[r:r + span, :]
    acc = jnp.zeros((ts, CONV_CH), F32) + bdw_ref[...]
    for j in range(CONV_WIDTH):
        r = (first + j) % SUBLANE
        base = first + j - r
        tap = buf[base:base + ts, :] if r == 0 else shifted[r, base:base + ts, :]
        acc = acc + wdw_ref[j:j + 1, :] * tap
    hn = _layer_norm(acc, gln_ref[...], bln_ref[...])
    hn = hn * jax.nn.sigmoid(hn)
    o = jnp.dot(hn.astype(BF16), wpw_ref[...], preferred_element_type=F32) + bpw_ref[...]
    o_ref[0] = o.astype(BF16)


def _conformer_conv(g, w_dw, b_dw, g_ln, b_ln, w_pw, b_pw):
    b, s, ch = g.shape
    ts = TS_CONV
    per = ts // CONV_HALO
    full = lambda shape: pl.BlockSpec(shape, lambda bi, i: (0,) * len(shape))
    return pl.pallas_call(
        functools.partial(_conv_body, ts=ts),
        grid=(b, s // ts),
        in_specs=[pl.BlockSpec((1, ts, ch), lambda bi, i: (bi, i, 0)),
                  pl.BlockSpec((1, CONV_HALO, ch), lambda bi, i: (bi, jnp.maximum(i * per - 1, 0), 0)),
                  full((CONV_WIDTH, ch)), full((1, ch)), full((1, ch)), full((1, ch)),
                  full((ch, ch)), full((1, ch))],
        out_specs=pl.BlockSpec((1, ts, ch), lambda bi, i: (bi, i, 0)),
        out_shape=jax.ShapeDtypeStruct((b, s, ch), BF16),
        scratch_shapes=[pltpu.VMEM((CONV_HALO + ts, ch), F32),
                        pltpu.VMEM((SUBLANE, CONV_HALO - SUBLANE + ts, ch), F32)],
        compiler_params=_cparams(("arbitrary", "arbitrary")),
        name="conformer_conv",
    )(g, g, w_dw, b_dw.reshape(1, ch), g_ln.reshape(1, ch), b_ln.reshape(1, ch),
      w_pw.astype(BF16), b_pw.reshape(1, ch))


def _outproj_body(om_ref, oc_ref, os_ref, x_ref, gt_ref, wa_ref, wb_ref, wc_ref, g_ref, b_ref, o_ref, *, alpha):
    mix = (jnp.dot(om_ref[0], wa_ref[...], preferred_element_type=F32)
           + jnp.dot(oc_ref[0], wb_ref[...], preferred_element_type=F32)
           + jnp.dot(os_ref[0], wc_ref[...], preferred_element_type=F32))
    y = alpha * x_ref[0] + (1.0 + gt_ref[0]) * mix
    o_ref[0] = _layer_norm(y, g_ref[...], b_ref[...])


def _prep_outproj_weights(w_o):
    d = w_o.shape[1]
    mla_out = MLA_HEADS * MLA_V
    swa_out = SWA_HEADS * SWA_HEAD_DIM
    wa = w_o[:mla_out].reshape(MLA_HEADS, MLA_V, d)
    wa = jnp.pad(wa, ((0, 0), (0, LANE - MLA_V), (0, 0))).reshape(MLA_HEADS * LANE, d)
    wb = w_o[mla_out:mla_out + CONV_CH]
    wc = w_o[mla_out + CONV_CH:].reshape(SWA_HEADS, SWA_HEAD_DIM, d)
    wc = jnp.pad(wc, ((0, 0), (0, LANE - SWA_HEAD_DIM), (0, 0))).reshape(SWA_HEADS * LANE, d)
    assert wc.shape[0] == SWA_HEADS * LANE and swa_out == SWA_HEADS * SWA_HEAD_DIM
    return wa.astype(BF16), wb.astype(BF16), wc.astype(BF16)


def _out_projection(o_mla, o_conv, o_swa, x, gt1, wa, wb, wc, g_ln, b_ln, alpha):
    b, s, d = x.shape
    tq = TQ_PROJ
    full = lambda shape: pl.BlockSpec(shape, lambda bi, i: (0,) * len(shape))
    tok = lambda w: pl.BlockSpec((1, tq, w), lambda bi, i: (bi, i, 0))
    return pl.pallas_call(
        functools.partial(_outproj_body, alpha=alpha),
        grid=(b, s // tq),
        in_specs=[tok(o_mla.shape[2]), tok(o_conv.shape[2]), tok(o_swa.shape[2]), tok(d),
                  pl.BlockSpec((1, 1, d), lambda bi, i: (bi, 0, 0)),
                  full(wa.shape), full(wb.shape), full(wc.shape), full((1, d)), full((1, d))],
        out_specs=tok(d),
        out_shape=jax.ShapeDtypeStruct((b, s, d), F32),
        compiler_params=_cparams(("arbitrary", "arbitrary")),
        name="out_projection",
    )(o_mla, o_conv, o_swa, x, gt1.reshape(b, 1, d), wa, wb, wc, g_ln.reshape(1, d), b_ln.reshape(1, d))


def _topk_rows(s, k, val_sc, idx_sc, payload=None):
    nrows = s.shape[0]
    rid = lax.broadcasted_iota(I32, s.shape, 0).astype(F32)
    for r in range(k):
        m = jnp.max(s, axis=0, keepdims=True)
        sel = jnp.min(jnp.where(s == m, rid, float(nrows)), axis=0, keepdims=True)
        hit = rid == sel
        val_sc[r:r + 1, :] = m
        if payload is None:
            idx_sc[r:r + 1, :] = sel
        else:
            idx_sc[r:r + 1, :] = jnp.max(jnp.where(hit, payload, -1.0), axis=0, keepdims=True)
        s = jnp.where(hit, -jnp.inf, s)


def _candidate_rows(k):
    return [(a, k // (a + 1)) for a in range(k)]


def _route_body(x_ref, sc_ref, sh_ref, wpq_ref, keys_ref, idx_ref, gate_ref,
                s1_sc, i1_sc, s2_sc, i2_sc, cand_sc, cidx_sc, top_sc, e_sc, idxT_sc, gateT_sc):
    k = PEER_TOPK
    ncand = sum(n for _, n in _candidate_rows(k))
    pad = cand_sc.shape[0] - ncand
    cand_sc[ncand:, :] = jnp.full((pad, cand_sc.shape[1]), -jnp.inf, F32)
    cidx_sc[ncand:, :] = jnp.full((pad, cand_sc.shape[1]), -1.0, F32)
    h = x_ref[0] * (1.0 + sc_ref[0]) + sh_ref[0]
    qp = jnp.dot(h.astype(BF16), wpq_ref[...], preferred_element_type=F32)
    for hd in range(PEER_HEADS):
        for half, (v_sc, i_sc) in enumerate(((s1_sc, i1_sc), (s2_sc, i2_sc))):
            col = (hd * 2 + half) * LANE
            qh = qp[:, col:col + LANE].astype(BF16)
            sc = lax.dot_general(keys_ref[hd * 2 + half], qh, _NT, preferred_element_type=F32)
            _topk_rows(sc, k, v_sc, i_sc)
        r0 = 0
        for a, n in _candidate_rows(k):
            cand_sc[r0:r0 + n, :] = s1_sc[a:a + 1, :] + s2_sc[0:n, :]
            cidx_sc[r0:r0 + n, :] = i1_sc[a:a + 1, :] * float(N_SUBKEYS) + i2_sc[0:n, :]
            r0 += n
        _topk_rows(cand_sc[...], k, top_sc, e_sc, payload=cidx_sc[...])
        top = top_sc[...]
        ex = jnp.exp(top - top[0:1, :])
        gateT_sc[hd * k:(hd + 1) * k, :] = ex / jnp.sum(ex, axis=0, keepdims=True)
        idxT_sc[hd * k:(hd + 1) * k, :] = (e_sc[...] * float(ROWS_PER_EXPERT)).astype(I32)
    idx_ref[...] = idxT_sc[...].T
    gate_ref[...] = gateT_sc[...].T


def _peer_route(x1, sc2, sh2, w_pq, keys):
    b, s, d = x1.shape
    tt = TT_ROUTE
    k = PEER_TOPK
    nq = w_pq.shape[1]
    per = s // tt
    ncand = sum(n for _, n in _candidate_rows(k))
    ncand_pad = -(-ncand // SUBLANE) * SUBLANE
    full = lambda shape: pl.BlockSpec(shape, lambda bi, i: (0,) * len(shape))
    vec = pl.BlockSpec((1, 1, d), lambda bi, i: (bi, 0, 0))
    out = pl.BlockSpec((tt, PEER_HK), lambda bi, i: (bi * per + i, 0))
    return pl.pallas_call(
        _route_body,
        grid=(b, per),
        in_specs=[pl.BlockSpec((1, tt, d), lambda bi, i: (bi, i, 0)), vec, vec, full((d, nq)),
                  full(keys.shape)],
        out_specs=[out, out],
        out_shape=[jax.ShapeDtypeStruct((b * s, PEER_HK), I32), jax.ShapeDtypeStruct((b * s, PEER_HK), F32)],
        scratch_shapes=[pltpu.VMEM((k, tt), F32)] * 4
                       + [pltpu.VMEM((ncand_pad, tt), F32)] * 2
                       + [pltpu.VMEM((k, tt), F32)] * 2
                       + [pltpu.VMEM((PEER_HK, tt), I32), pltpu.VMEM((PEER_HK, tt), F32)],
        compiler_params=_cparams(("arbitrary", "arbitrary")),
        name="peer_route",
    )(x1, sc2.reshape(b, 1, d), sh2.reshape(b, 1, d), w_pq, keys)


def _pack_table(tab):
    n, d = tab.shape
    t16 = lax.bitcast_convert_type(tab.astype(BF16), jnp.uint16).astype(jnp.uint32)
    words = (t16[:, :d // 2] << 16) | t16[:, d // 2:]
    return lax.bitcast_convert_type(words, I32).reshape(n * ROWS_PER_EXPERT, LANE)


def _unpack(words):
    hi = lax.bitcast_convert_type(words & jnp.int32(-65536), F32)
    lo = lax.bitcast_convert_type(words << 16, F32)
    return hi, lo


def _load_table(tab_hbm, tab_vmem, sem):
    @pl.when(pl.program_id(0) == 0)
    def _():
        cp = pltpu.make_async_copy(tab_hbm, tab_vmem, sem)
        cp.start()
        cp.wait()


def _gather_rows(idx_ref, tab_vmem, t, gbuf):
    idx_t = idx_ref.at[pl.ds(t * PEER_HK, PEER_HK)]
    for g in range(PEER_HK // GATHER_GROUP):
        idx_g = idx_t.at[pl.ds(g * GATHER_GROUP, GATHER_GROUP)]
        for k in range(GATHER_GROUP):
            j = g * GATHER_GROUP + k
            r = pl.multiple_of(idx_g[k], ROWS_PER_EXPERT)
            gbuf[j * ROWS_PER_EXPERT:(j + 1) * ROWS_PER_EXPERT, :] = tab_vmem[pl.ds(r, ROWS_PER_EXPERT), :]


def _pipelined_tokens(tn, gather, compute, bufs):
    p = len(bufs) // 2
    set_a, set_b = bufs[:p], bufs[p:]
    for k in range(p):
        gather(k, set_a[k])

    def body(i, carry):
        t = 2 * p * i
        for k in range(p):
            gather(t + p + k, set_b[k])
            compute(t + k, set_a[k])
        for k in range(p):
            gather(jnp.minimum(t + 2 * p + k, tn - 1), set_a[k])
            compute(t + p + k, set_b[k])
        return carry

    lax.fori_loop(0, tn // (2 * p), body, 0)


def _peer_u_body(idx_ref, x_ref, sc_ref, sh_ref, gate_ref, tab_hbm, w_ref, tab_vmem, sem, hs, *bufs, tn):
    _load_table(tab_hbm, tab_vmem, sem)
    scale = 1.0 + sc_ref[0]
    shift = sh_ref[0]

    nchunk = 2 * ROWS_PER_EXPERT
    for c in range(nchunk):
        lanes = slice(c * LANE, (c + 1) * LANE)
        hs[pl.ds(c, tn, stride=nchunk), :] = x_ref[:, lanes] * scale[:, lanes] + shift[:, lanes]

    def compute(t, gbuf):
        h = hs[pl.ds(pl.multiple_of(t * nchunk, nchunk), nchunk), :]
        acc = jnp.zeros((PEER_HK, LANE), F32)
        for c in range(ROWS_PER_EXPERT):
            hi, lo = _unpack(gbuf[pl.ds(c, PEER_HK, stride=ROWS_PER_EXPERT), :])
            acc = acc + hi * h[c:c + 1, :] + lo * h[ROWS_PER_EXPERT + c:ROWS_PER_EXPERT + c + 1, :]
        srow = jnp.sum(acc.T, axis=0, keepdims=True)
        act = 0.5 * srow * (1.0 + lax.erf(srow * (2.0 ** -0.5)))
        w_ref[pl.ds(t, 1), :] = gate_ref[pl.ds(t, 1), :] * act

    _pipelined_tokens(tn, functools.partial(_gather_rows, idx_ref, tab_vmem), compute, bufs)


def _peer_v_body(idx_ref, w_ref, tab_hbm, y_ref, tab_vmem, sem, *bufs, tn):
    _load_table(tab_hbm, tab_vmem, sem)

    def compute(t, gbuf):
        wrow = w_ref[pl.ds(t, 1), :]
        wrep = jnp.broadcast_to(wrow, (LANE, PEER_HK)).T
        his, los = [], []
        for c in range(ROWS_PER_EXPERT):
            hi, lo = _unpack(gbuf[pl.ds(c, PEER_HK, stride=ROWS_PER_EXPERT), :])
            his.append(jnp.sum(wrep * hi, axis=0, keepdims=True))
            los.append(jnp.sum(wrep * lo, axis=0, keepdims=True))
        y_ref[pl.ds(t, 1), :] = jnp.concatenate(his + los, axis=1)

    _pipelined_tokens(tn, functools.partial(_gather_rows, idx_ref, tab_vmem), compute, bufs)


def _peer_experts(idx, gate, x1, sc2, sh2, u_pack, v_pack):
    b, s, d = x1.shape
    t = b * s
    tn = TN_PEER
    per = s // tn
    assert d == 2 * ROWS_PER_EXPERT * LANE
    idx_flat = idx.reshape(t * PEER_HK)
    idx_spec = pl.BlockSpec((tn * PEER_HK,), lambda i: (i,), memory_space=pltpu.SMEM)
    tokrows = pl.BlockSpec((tn, d), lambda i: (i, 0))
    tokhk = pl.BlockSpec((tn, PEER_HK), lambda i: (i, 0))
    vec = pl.BlockSpec((1, 1, d), lambda i: (i // per, 0, 0))
    table = pl.BlockSpec(memory_space=pl.ANY)
    gbuf = pltpu.VMEM((PEER_HK * ROWS_PER_EXPERT, LANE), I32)
    scratch = [pltpu.VMEM(u_pack.shape, I32), pltpu.SemaphoreType.DMA(())]
    gbufs_in, gbufs_out = [gbuf] * (2 * PEER_PIPE_IN), [gbuf] * (2 * PEER_PIPE_OUT)
    w = pl.pallas_call(
        functools.partial(_peer_u_body, tn=tn),
        grid=(t // tn,),
        in_specs=[idx_spec, tokrows, vec, vec, tokhk, table],
        out_specs=tokhk,
        out_shape=jax.ShapeDtypeStruct((t, PEER_HK), F32),
        scratch_shapes=scratch + [pltpu.VMEM((tn * d // LANE, LANE), F32)] + gbufs_in,
        compiler_params=_cparams(("arbitrary",), VMEM_LIMIT_TABLE),
        name="peer_expert_in",
    )(idx_flat, x1.reshape(t, d), sc2.reshape(b, 1, d), sh2.reshape(b, 1, d), gate, u_pack)
    y = pl.pallas_call(
        functools.partial(_peer_v_body, tn=tn),
        grid=(t // tn,),
        in_specs=[idx_spec, tokhk, table],
        out_specs=tokrows,
        out_shape=jax.ShapeDtypeStruct((t, d), F32),
        scratch_shapes=scratch + gbufs_out,
        compiler_params=_cparams(("arbitrary",), VMEM_LIMIT_TABLE),
        name="peer_expert_out",
    )(idx_flat, w, v_pack)
    return y.reshape(b, s, d)


def _resnorm_body(x_ref, y_ref, gt_ref, g_ref, b_ref, o_ref, *, alpha):
    v = alpha * x_ref[0] + (1.0 + gt_ref[0]) * y_ref[0]
    o_ref[0] = _layer_norm(v, g_ref[...], b_ref[...])


def _residual_norm(x, y, gt, g_ln, b_ln, alpha):
    b, s, d = x.shape
    tq = TQ_PROJ
    tok = pl.BlockSpec((1, tq, d), lambda bi, i: (bi, i, 0))
    full = pl.BlockSpec((1, d), lambda bi, i: (0, 0))
    return pl.pallas_call(
        functools.partial(_resnorm_body, alpha=alpha),
        grid=(b, s // tq),
        in_specs=[tok, tok, pl.BlockSpec((1, 1, d), lambda bi, i: (bi, 0, 0)), full, full],
        out_specs=tok,
        out_shape=jax.ShapeDtypeStruct((b, s, d), F32),
        compiler_params=_cparams(("arbitrary", "arbitrary")),
        name="residual_norm",
    )(x, y, gt.reshape(b, 1, d), g_ln.reshape(1, d), b_ln.reshape(1, d))


def kernel(x, c, w_ada, b_ada, w_in, g_q, w_uq, g_kv, w_ukv, w_dw, b_dw, g_conv, b_conv, w_pw, b_pw, sinks,
           w_o, g_ln1, b_ln1, w_pq, sub_keys, u_tab, v_tab, g_ln2, b_ln2):
    depth = w_ada.shape[0]
    b, s, d = x.shape
    alpha = float((2 * depth) ** 0.25)
    cos_t, sin_t = _rope_tables(s)
    mod = _ada_mod(c, w_ada, b_ada)
    for l in range(depth):
        sh1, sc1, gt1, sh2, sc2, gt2 = [mod[l, :, i * d:(i + 1) * d] for i in range(6)]
        w_in_p, w_q2, w_kv2 = _prep_inproj_weights(w_in[l], w_uq[l], w_ukv[l])
        q, k, v, g, qs, ks, vs = _in_projection(x, sc1, sh1, w_in_p, g_q[l], w_q2, g_kv[l], w_kv2, cos_t, sin_t)
        o_mla = _mla_attention(q, k, v)
        o_swa = _swa_attention(qs, ks, vs, sinks[l])
        o_conv = _conformer_conv(g, w_dw[l], b_dw[l], g_conv[l], b_conv[l], w_pw[l], b_pw[l])
        wa, wb, wc = _prep_outproj_weights(w_o[l])
        x = _out_projection(o_mla, o_conv, o_swa, x, gt1, wa, wb, wc, g_ln1[l], b_ln1[l], alpha)

        keys = sub_keys[l].reshape(PEER_HEADS * 2, N_SUBKEYS, -1).astype(BF16)
        idx, gate = _peer_route(x, sc2, sh2, w_pq[l].astype(BF16), keys)
        y = _peer_experts(idx, gate, x, sc2, sh2, _pack_table(u_tab[l]), _pack_table(v_tab[l]))
        x = _residual_norm(x, y, gt2, g_ln2[l], b_ln2[l], alpha)
    return x
```
